```python
import math
import jax
import jax.numpy as jnp
from jax import lax
import numpy as np

D_MODEL = 2048
BATCH = 4
SEQ = 4096
DEPTH = 1
DEC_BATCH = 4
DEC_SEQ = 8192
PAST_LEN = 128

ATTN_HEADS = 8
ATTN_QK_DIM = 64
ATTN_V_DIM = 2 * ATTN_QK_DIM
ATTN_QK_WIDTH = ATTN_HEADS * 2 * ATTN_QK_DIM
ATTN_WIDTH = ATTN_HEADS * ATTN_V_DIM
HYENA_WIDTH = D_MODEL // 2
HYENA_PROJ = 3
HYENA_EMB_DIM = 33
HYENA_BANDS = (HYENA_EMB_DIM - 1) // 2
FILTER_WIDTH = 64
DECAY_TARGET = 1e-2
FAST_DECAY_PCT = 0.3
SLOW_DECAY_PCT = 1.5
SHORT_CONV = 3
N_BRANCH = 2
D_FF = 5632
ROPE_THETA = 10000.0
NORM_EPS = 1e-6
Q_BLOCK = 128
IN_WIDTH = 2 * ATTN_QK_WIDTH + ATTN_WIDTH + HYENA_PROJ * HYENA_WIDTH + N_BRANCH * D_MODEL

kernel_name = "hybrid_diffattn_hyena_encoder"


def rmsnorm(x, g):
    xf = x.astype(jnp.float32)
    xf = xf * lax.rsqrt(jnp.mean(xf * xf, axis=-1, keepdims=True) + NORM_EPS)
    return (xf * g.astype(jnp.float32)).astype(x.dtype)


def dwconv3(x, w, b):
    L = x.shape[1]
    xp = jnp.pad(x, ((0, 0), (1, 1), (0, 0)))
    return xp[:, 0:L] * w[0] + xp[:, 1:L + 1] * w[1] + xp[:, 2:L + 2] * w[2] + b


def rope(x):
    L, d = x.shape[1], x.shape[-1]
    inv = ROPE_THETA ** (-jnp.arange(0, d, 2, dtype=jnp.float32) / d)
    ang = jnp.arange(L, dtype=jnp.float32)[:, None] * inv[None]
    ang = jnp.concatenate([ang, ang], -1)[:, None, None, :]
    xf = x.astype(jnp.float32)
    rot = jnp.concatenate([-xf[..., d // 2:], xf[..., :d // 2]], -1)
    return (xf * jnp.cos(ang) + rot * jnp.sin(ang)).astype(x.dtype)


def diff_attention(q, k, v, lam):
    B, L, H = q.shape[:3]
    nb = L // Q_BLOCK
    scale = ATTN_QK_DIM ** -0.5
    qb = jnp.moveaxis(q.reshape(B, nb, Q_BLOCK, H, 2, ATTN_QK_DIM), 1, 0)
    vf = v.astype(jnp.float32)

    def block(qi):
        s = jnp.einsum("bqhmd,bkhmd->bhmqk", qi, k, preferred_element_type=jnp.float32) * scale
        p = jax.nn.softmax(s, axis=-1)
        a = p[:, :, 0] - lam * p[:, :, 1]
        return jnp.einsum("bhqk,bkhd->bqhd", a, vf).astype(v.dtype)

    o = lax.map(block, qb)
    return jnp.moveaxis(o, 0, 1).reshape(B, L, H, ATTN_V_DIM)


def hyena_filters(L, w1, b1, w2, b2, w3, b3, w4, freq):
    t = jnp.linspace(0.0, 1.0, L, dtype=jnp.float32)[:, None]
    w = 2.0 * math.pi * jnp.arange(L, dtype=jnp.float32)[:, None] / L
    f = jnp.linspace(1e-4, HYENA_BANDS - 1, HYENA_BANDS, dtype=jnp.float32)[None]
    z = jnp.concatenate([t, jnp.cos(f * w), -jnp.sin(f * w)], -1)
    h = jnp.sin(freq * (z @ w1 + b1))
    h = jnp.sin(freq * (h @ w2 + b2))
    h = jnp.sin(freq * (h @ w3 + b3))
    h = (h @ w4).astype(jnp.float32).reshape(L, 2, HYENA_WIDTH)
    max_decay = math.log(DECAY_TARGET) / FAST_DECAY_PCT
    min_decay = math.log(DECAY_TARGET) / SLOW_DECAY_PCT
    deltas = jnp.abs(jnp.linspace(min_decay, max_decay, HYENA_WIDTH, dtype=jnp.float32))
    h = h * jnp.exp(-t * deltas[None])[:, None, :]
    return h[:, 0], h[:, 1]


def fftconv_bidir(u, h_fwd, h_bwd, bias):
    L, C = h_fwd.shape
    n = 2 * L
    filt = jnp.concatenate([h_fwd, jnp.zeros((1, C), jnp.float32), h_bwd[1:][::-1]], 0)
    Hf = jnp.fft.rfft(filt, n=n, axis=0)
    U = jnp.fft.rfft(u.astype(jnp.float32), n=n, axis=1)
    y = jnp.fft.irfft(U * Hf[None], n=n, axis=1)[:, :L]
    return (y + u.astype(jnp.float32) * bias.astype(jnp.float32)).astype(u.dtype)


def encoder_layer(x, lambda_init, norm1, w_in, in_conv_w, in_conv_b, gate_b,
                  lambda_q1, lambda_k1, lambda_q2, lambda_k2, subln_g,
                  filt_w1, filt_b1, filt_w2, filt_b2, filt_w3, filt_b3, filt_w4, filt_freq,
                  hyena_bias, w_attn_out, w_hyena_out, w_out,
                  norm2, w_up, ffn_conv_w, ffn_conv_b, w_down):
    B, L, _ = x.shape
    hn = rmsnorm(x, norm1)
    proj = hn @ w_in
    o1 = ATTN_QK_WIDTH
    o2 = o1 + ATTN_QK_WIDTH
    o3 = o2 + ATTN_WIDTH
    o4 = o3 + HYENA_PROJ * HYENA_WIDTH
    q, k, v, hy, gl = jnp.split(proj, [o1, o2, o3, o4], axis=-1)

    q = rope(q.reshape(B, L, ATTN_HEADS, 2, ATTN_QK_DIM))
    k = rope(k.reshape(B, L, ATTN_HEADS, 2, ATTN_QK_DIM))
    v = v.reshape(B, L, ATTN_HEADS, ATTN_V_DIM)
    lam = (jnp.exp(jnp.sum(lambda_q1.astype(jnp.float32) * lambda_k1.astype(jnp.float32)))
           - jnp.exp(jnp.sum(lambda_q2.astype(jnp.float32) * lambda_k2.astype(jnp.float32)))
           + lambda_init)
    attn = diff_attention(q, k, v, lam)
    attn = (rmsnorm(attn, subln_g) * (1.0 - lambda_init)).reshape(B, L, ATTN_WIDTH)

    hy = dwconv3(hy, in_conv_w, in_conv_b)
    x0, x1, hv = jnp.split(hy, HYENA_PROJ, axis=-1)
    h_fwd, h_bwd = hyena_filters(L, filt_w1, filt_b1, filt_w2, filt_b2, filt_w3, filt_b3,
                                 filt_w4, filt_freq)
    hyena = fftconv_bidir(hv * x1, h_fwd, h_bwd, hyena_bias) * x0

    gates = jax.nn.sigmoid(gl.reshape(B, L, N_BRANCH, D_MODEL) + gate_b.reshape(N_BRANCH, D_MODEL))
    merged = gates[:, :, 0] * (attn @ w_attn_out) + gates[:, :, 1] * (hyena @ w_hyena_out)
    x = x + merged @ w_out

    hn = rmsnorm(x, norm2)
    u = dwconv3(hn @ w_up, ffn_conv_w, ffn_conv_b)
    ug, uv = jnp.split(u, 2, axis=-1)
    return x + (jax.nn.silu(ug) * uv) @ w_down


def setup_inputs(seed: int = 0) -> dict:
    key = jax.random.key(seed)
    ks = jax.random.split(key, 40)
    counter = [0]

    def nrm(shape, scale):
        k = ks[counter[0]]
        counter[0] += 1
        return jax.random.normal(k, shape, jnp.float32) * scale

    def gain(shape):
        return 1.0 + nrm(shape, 0.02)

    C = HYENA_WIDTH
    inputs = {}
    inputs["x_prompt"] = nrm((BATCH, SEQ, D_MODEL), 1.0)
    inputs["x_sample"] = nrm((DEC_BATCH, DEC_SEQ, D_MODEL), 1.0)
    inputs["norm1"] = gain((DEPTH, D_MODEL))
    inputs["w_in"] = nrm((DEPTH, D_MODEL, IN_WIDTH), D_MODEL ** -0.5)
    inputs["in_conv_w"] = nrm((DEPTH, SHORT_CONV, HYENA_PROJ * C), 0.5)
    inputs["in_conv_b"] = nrm((DEPTH, HYENA_PROJ * C), 0.02)
    inputs["gate_b"] = nrm((DEPTH, N_BRANCH * D_MODEL), 0.1)
    inputs["lambda_q1"] = nrm((DEPTH, ATTN_QK_DIM), 0.1)
    inputs["lambda_k1"] = nrm((DEPTH, ATTN_QK_DIM), 0.1)
    inputs["lambda_q2"] = nrm((DEPTH, ATTN_QK_DIM), 0.1)
    inputs["lambda_k2"] = nrm((DEPTH, ATTN_QK_DIM), 0.1)
    inputs["subln_g"] = gain((DEPTH, ATTN_V_DIM))
    inputs["filt_w1"] = nrm((DEPTH, HYENA_EMB_DIM, FILTER_WIDTH), HYENA_EMB_DIM ** -0.5)
    inputs["filt_b1"] = nrm((DEPTH, FILTER_WIDTH), 0.1)
    inputs["filt_w2"] = nrm((DEPTH, FILTER_WIDTH, FILTER_WIDTH), FILTER_WIDTH ** -0.5)
    inputs["filt_b2"] = nrm((DEPTH, FILTER_WIDTH), 0.1)
    inputs["filt_w3"] = nrm((DEPTH, FILTER_WIDTH, FILTER_WIDTH), FILTER_WIDTH ** -0.5)
    inputs["filt_b3"] = nrm((DEPTH, FILTER_WIDTH), 0.1)
    inputs["filt_w4"] = nrm((DEPTH, FILTER_WIDTH, 2 * C), 0.1 * FILTER_WIDTH ** -0.5)
    inputs["filt_freq"] = 1.0 + nrm((DEPTH, FILTER_WIDTH), 0.1)
    inputs["hyena_bias"] = nrm((DEPTH, C), 0.1)
    inputs["w_attn_out"] = nrm((DEPTH, ATTN_WIDTH, D_MODEL), ATTN_WIDTH ** -0.5)
    inputs["w_hyena_out"] = nrm((DEPTH, C, D_MODEL), C ** -0.5)
    inputs["w_out"] = nrm((DEPTH, D_MODEL, D_MODEL), D_MODEL ** -0.5)
    inputs["norm2"] = gain((DEPTH, D_MODEL))
    inputs["w_up"] = nrm((DEPTH, D_MODEL, 2 * D_FF), D_MODEL ** -0.5)
    inputs["ffn_conv_w"] = nrm((DEPTH, SHORT_CONV, 2 * D_FF), 0.5)
    inputs["ffn_conv_b"] = nrm((DEPTH, 2 * D_FF), 0.02)
    inputs["w_down"] = nrm((DEPTH, D_FF, D_MODEL), D_FF ** -0.5)
    inputs["norm_f"] = gain((D_MODEL,))
    return inputs


def reference(x_prompt, x_sample, norm1, w_in, in_conv_w, in_conv_b, gate_b,
              lambda_q1, lambda_k1, lambda_q2, lambda_k2, subln_g,
              filt_w1, filt_b1, filt_w2, filt_b2, filt_w3, filt_b3, filt_w4, filt_freq,
              hyena_bias, w_attn_out, w_hyena_out, w_out,
              norm2, w_up, ffn_conv_w, ffn_conv_b, w_down, norm_f):
    params = (norm1, w_in, in_conv_w, in_conv_b, gate_b,
              lambda_q1, lambda_k1, lambda_q2, lambda_k2, subln_g,
              filt_w1, filt_b1, filt_w2, filt_b2, filt_w3, filt_b3, filt_w4, filt_freq,
              hyena_bias, w_attn_out, w_hyena_out, w_out,
              norm2, w_up, ffn_conv_w, ffn_conv_b, w_down)

    def trunk(x):
        for i in range(DEPTH):
            lambda_init = 0.8 - 0.6 * math.exp(-0.3 * i)
            x = encoder_layer(x, lambda_init, *[p[i] for p in params])
        return rmsnorm(x, norm_f)

    y_prompt = trunk(x_prompt)
    y_sample = trunk(x_sample)
    return (y_prompt, y_sample)
```

```python
import functools
import math

import numpy as np
import jax
import jax.numpy as jnp
from jax import lax
from jax.experimental import pallas as pl
from jax.experimental.pallas import tpu as pltpu

ATTN_HEADS = 8
ATTN_QK_DIM = 64
ATTN_V_DIM = 2 * ATTN_QK_DIM
ATTN_QK_WIDTH = ATTN_HEADS * 2 * ATTN_QK_DIM
ATTN_WIDTH = ATTN_HEADS * ATTN_V_DIM
HYENA_EMB_DIM = 33
HYENA_BANDS = (HYENA_EMB_DIM - 1) // 2
DECAY_TARGET = 1e-2
FAST_DECAY_PCT = 0.3
SLOW_DECAY_PCT = 1.5
ROPE_THETA = 10000.0
NORM_EPS = 1e-6

LANES = 128
BF16_SUBLANES = 16
VMEM_LIMIT_BYTES = 56 * 1024 * 1024

DFT_INNER = LANES

F32 = jnp.float32
BF16 = jnp.bfloat16


def _cparams(*sem):
    return pltpu.CompilerParams(dimension_semantics=sem, vmem_limit_bytes=VMEM_LIMIT_BYTES)


def _pick(n, pref):
    t = min(n, pref)
    while n % t:
        t //= 2
    return t


def _rmsnorm_kernel(x_ref, g_ref, o_ref):
    x = x_ref[...]
    ms = jnp.mean(x * x, axis=-1, keepdims=True)
    o_ref[...] = (x * lax.rsqrt(ms + NORM_EPS) * g_ref[...]).astype(o_ref.dtype)


def _rmsnorm(x, g, tm):
    t, d = x.shape
    return pl.pallas_call(
        _rmsnorm_kernel,
        grid=(t // tm,),
        in_specs=[pl.BlockSpec((tm, d), lambda i: (i, 0)),
                  pl.BlockSpec((1, d), lambda i: (0, 0))],
        out_specs=pl.BlockSpec((tm, d), lambda i: (i, 0)),
        out_shape=jax.ShapeDtypeStruct((t, d), BF16),
        compiler_params=_cparams("parallel"),
        name="rmsnorm_in",
    )(x, g.reshape(1, d))


def _proj_plain_kernel(h_ref, w_ref, o_ref):
    o_ref[...] = jnp.dot(h_ref[...], w_ref[...], preferred_element_type=F32).astype(o_ref.dtype)


def _proj_plain(h, w, tm, tn):
    t, d = h.shape
    n = w.shape[1]
    return pl.pallas_call(
        _proj_plain_kernel,
        grid=(t // tm, n // tn),
        in_specs=[pl.BlockSpec((tm, d), lambda i, j: (i, 0)),
                  pl.BlockSpec((d, tn), lambda i, j: (0, j))],
        out_specs=pl.BlockSpec((tm, tn), lambda i, j: (i, j)),
        out_shape=jax.ShapeDtypeStruct((t, n), BF16),
        compiler_params=_cparams("parallel", "arbitrary"),
        name="proj_plain",
    )(h, w)


def _proj_gate_kernel(h_ref, w_ref, b_ref, o_ref):
    z = jnp.dot(h_ref[...], w_ref[...], preferred_element_type=F32) + b_ref[...]
    o_ref[...] = (1.0 / (1.0 + jnp.exp(-z))).astype(o_ref.dtype)


def _proj_gate(h, w, b, tm, tn):
    t, d = h.shape
    n = w.shape[1]
    return pl.pallas_call(
        _proj_gate_kernel,
        grid=(t // tm, n // tn),
        in_specs=[pl.BlockSpec((tm, d), lambda i, j: (i, 0)),
                  pl.BlockSpec((d, tn), lambda i, j: (0, j)),
                  pl.BlockSpec((1, tn), lambda i, j: (0, j))],
        out_specs=pl.BlockSpec((tm, tn), lambda i, j: (i, j)),
        out_shape=jax.ShapeDtypeStruct((t, n), BF16),
        compiler_params=_cparams("parallel", "arbitrary"),
        name="proj_gate",
    )(h, w, b.reshape(1, n))


def _proj_rope_kernel(h_ref, w_ref, cos_ref, sin_ref, o_ref, *, q_scale):
    acc = jnp.dot(h_ref[...], w_ref[...], preferred_element_type=F32)
    cos = cos_ref[...]
    sin = sin_ref[...]
    lane = lax.broadcasted_iota(jnp.int32, cos.shape, 1)
    first_half = (lane % ATTN_QK_DIM) < (ATTN_QK_DIM // 2)
    scale = jnp.where(pl.program_id(1) == 0, q_scale, 1.0).astype(F32)
    half = ATTN_QK_DIM // 2
    for g in range(acc.shape[1] // LANES):
        x = acc[:, g * LANES:(g + 1) * LANES]
        up = pltpu.roll(x, LANES - half, 1)
        down = pltpu.roll(x, half, 1)
        rot = jnp.where(first_half, up, down)
        o_ref[:, g * LANES:(g + 1) * LANES] = ((x * cos + rot * sin) * scale).astype(o_ref.dtype)


def _proj_rope(h, w, cos, sin, seq_len, tm):
    t, d = h.shape
    n = w.shape[1]
    tn = ATTN_QK_WIDTH
    nblk = seq_len // tm
    kern = functools.partial(_proj_rope_kernel, q_scale=ATTN_QK_DIM ** -0.5)
    return pl.pallas_call(
        kern,
        grid=(t // tm, n // tn),
        in_specs=[pl.BlockSpec((tm, d), lambda i, j: (i, 0)),
                  pl.BlockSpec((d, tn), lambda i, j: (0, j)),
                  pl.BlockSpec((tm, LANES), lambda i, j: (i % nblk, 0)),
                  pl.BlockSpec((tm, LANES), lambda i, j: (i % nblk, 0))],
        out_specs=pl.BlockSpec((tm, tn), lambda i, j: (i, j)),
        out_shape=jax.ShapeDtypeStruct((t, n), BF16),
        compiler_params=_cparams("parallel", "arbitrary"),
        name="proj_rope",
    )(h, w, cos, sin)


def _rope_tables(seq_len):
    d = ATTN_QK_DIM
    inv = ROPE_THETA ** (-jnp.arange(0, d, 2, dtype=F32) / d)
    ang = jnp.arange(seq_len, dtype=F32)[:, None] * inv[None]
    ang = jnp.concatenate([ang, ang], -1)
    sign = jnp.concatenate([-jnp.ones((d // 2,), F32), jnp.ones((d // 2,), F32)])
    cos = jnp.cos(ang)
    sin = jnp.sin(ang) * sign
    reps = LANES // d
    return jnp.tile(cos, (1, reps)), jnp.tile(sin, (1, reps))


def _attn_kernel(q_ref, k_ref, v_ref, lq1_ref, lk1_ref, lq2_ref, lk2_ref, g_ref, o_ref,
                 m0_ref, m1_ref, acc0_ref, acc1_ref, *, tk, lambda_init):
    tq = q_ref.shape[0]
    seq = k_ref.shape[0]
    q = q_ref[...]
    lane = lax.broadcasted_iota(jnp.int32, q.shape, 1)
    zero = jnp.zeros_like(q)
    qa = jnp.where(lane < ATTN_QK_DIM, q, zero)
    qb = jnp.where(lane >= ATTN_QK_DIM, q, zero)

    m0_ref[...] = jnp.full(m0_ref.shape, -jnp.inf, F32)
    m1_ref[...] = jnp.full(m1_ref.shape, -jnp.inf, F32)
    acc0_ref[...] = jnp.zeros(acc0_ref.shape, F32)
    acc1_ref[...] = jnp.zeros(acc1_ref.shape, F32)

    ones_col = (lax.broadcasted_iota(jnp.int32, (tk, LANES), 1) == 0).astype(BF16)
    nt = (((1,), (1,)), ((), ()))

    def one_map(qm, kc, vext, m_ref, acc_ref):
        s = lax.dot_general(qm, kc, nt, preferred_element_type=F32)
        m_old = m_ref[...]
        m_new = jnp.maximum(m_old, jnp.max(s, axis=1, keepdims=True))
        alpha = jnp.exp(m_old - m_new)
        p = jnp.exp(s - m_new).astype(BF16)
        acc_ref[...] = alpha * acc_ref[...] + jnp.dot(p, vext, preferred_element_type=F32)
        m_ref[...] = m_new

    def body(j, carry):
        start = pl.multiple_of(j * tk, tk)
        kc = k_ref[pl.ds(start, tk), :]
        vext = jnp.concatenate([v_ref[pl.ds(start, tk), :], ones_col], axis=1)
        one_map(qa, kc, vext, m0_ref, acc0_ref)
        one_map(qb, kc, vext, m1_ref, acc1_ref)
        return carry

    lax.fori_loop(0, seq // tk, body, 0)

    lam = (jnp.exp(jnp.sum(lq1_ref[...] * lk1_ref[...], axis=1, keepdims=True))
           - jnp.exp(jnp.sum(lq2_ref[...] * lk2_ref[...], axis=1, keepdims=True))
           + lambda_init)
    a0 = acc0_ref[...]
    a1 = acc1_ref[...]
    o0 = a0[:, :LANES] / a0[:, LANES:LANES + 1]
    o1 = a1[:, :LANES] / a1[:, LANES:LANES + 1]
    o = o0 - lam * o1
    ms = jnp.mean(o * o, axis=-1, keepdims=True)
    o = o * lax.rsqrt(ms + NORM_EPS) * g_ref[...]
    o_ref[...] = (o * (1.0 - lambda_init)).astype(o_ref.dtype)


def _attention(qk, v, lq1, lk1, lq2, lk2, subln_g, batch, seq_len, lambda_init, tq, tk):
    kern = functools.partial(_attn_kernel, tk=tk, lambda_init=lambda_init)
    small = pl.BlockSpec((1, ATTN_QK_DIM), lambda b, h, i: (0, 0))
    return pl.pallas_call(
        kern,
        grid=(batch, ATTN_HEADS, seq_len // tq),
        in_specs=[pl.BlockSpec((None, tq, LANES), lambda b, h, i: (b, i, h)),
                  pl.BlockSpec((None, seq_len, LANES), lambda b, h, i: (b, 0, ATTN_HEADS + h)),
                  pl.BlockSpec((None, seq_len, LANES), lambda b, h, i: (b, 0, h)),
                  small, small, small, small,
                  pl.BlockSpec((1, LANES), lambda b, h, i: (0, 0))],
        out_specs=pl.BlockSpec((None, tq, LANES), lambda b, h, i: (b, i, h)),
        out_shape=jax.ShapeDtypeStruct((batch, seq_len, ATTN_WIDTH), BF16),
        scratch_shapes=[pltpu.VMEM((tq, 1), F32), pltpu.VMEM((tq, 1), F32),
                        pltpu.VMEM((tq, 2 * LANES), F32), pltpu.VMEM((tq, 2 * LANES), F32)],
        compiler_params=_cparams("parallel", "parallel", "arbitrary"),
        name="diff_attention",
    )(qk, qk, v, lq1.reshape(1, -1), lk1.reshape(1, -1), lq2.reshape(1, -1), lk2.reshape(1, -1),
      subln_g.reshape(1, -1))


def _conv3_rows(e, w, n_rows, halo):
    tot = e.shape[0]
    prev = pltpu.roll(e, 1, 0)
    nxt = pltpu.roll(e, tot - 1, 0)
    c = prev * w[0:1, :] + e * w[1:2, :] + nxt * w[2:3, :]
    return c[halo:halo + n_rows, :]


def _hyena_prep_kernel(prev_ref, main_ref, next_ref, w_ref, b_ref, u_ref, x0_ref, *, blocks_per_seq):
    i = pl.program_id(0)
    first = (i % blocks_per_seq) == 0
    last = (i % blocks_per_seq) == blocks_per_seq - 1
    halo = prev_ref.shape[0]
    tm = main_ref.shape[0]
    prev = jnp.where(first, 0.0, prev_ref[...].astype(F32))
    nxt = jnp.where(last, 0.0, next_ref[...].astype(F32))
    e = jnp.concatenate([prev, main_ref[...].astype(F32), nxt], axis=0)
    c = _conv3_rows(e, w_ref[...], tm, halo) + b_ref[...]
    width = u_ref.shape[1]
    x0 = c[:, :width]
    x1 = c[:, width:2 * width]
    hv = c[:, 2 * width:]
    u_ref[...] = (hv * x1).astype(u_ref.dtype)
    x0_ref[...] = x0.astype(x0_ref.dtype)


def _hyena_prep(hy, conv_w, conv_b, seq_len, tm):
    t, n = hy.shape
    width = n // 3
    halo = BF16_SUBLANES
    r = tm // halo
    nb = t // halo
    kern = functools.partial(_hyena_prep_kernel, blocks_per_seq=seq_len // tm)
    return pl.pallas_call(
        kern,
        grid=(t // tm,),
        in_specs=[pl.BlockSpec((halo, n), lambda i: (jnp.maximum(i * r - 1, 0), 0)),
                  pl.BlockSpec((tm, n), lambda i: (i, 0)),
                  pl.BlockSpec((halo, n), lambda i: (jnp.minimum((i + 1) * r, nb - 1), 0)),
                  pl.BlockSpec((3, n), lambda i: (0, 0)),
                  pl.BlockSpec((1, n), lambda i: (0, 0))],
        out_specs=[pl.BlockSpec((tm, width), lambda i: (i, 0)),
                   pl.BlockSpec((tm, width), lambda i: (i, 0))],
        out_shape=[jax.ShapeDtypeStruct((t, width), BF16),
                   jax.ShapeDtypeStruct((t, width), BF16)],
        compiler_params=_cparams("parallel"),
        name="hyena_prep",
    )(hy, hy, hy, conv_w, conv_b.reshape(1, n))


def _filter_kernel(z_ref, w1_ref, b1_ref, w2_ref, b2_ref, w3_ref, b3_ref, w4_ref, f_ref, d_ref, o_ref):
    hp = lax.Precision.HIGHEST
    z = z_ref[...]
    f = f_ref[...]
    h = jnp.sin(f * (jnp.dot(z, w1_ref[...], precision=hp, preferred_element_type=F32) + b1_ref[...]))
    h = jnp.sin(f * (jnp.dot(h, w2_ref[...], precision=hp, preferred_element_type=F32) + b2_ref[...]))
    h = jnp.sin(f * (jnp.dot(h, w3_ref[...], precision=hp, preferred_element_type=F32) + b3_ref[...]))
    h = jnp.dot(h, w4_ref[...], precision=hp, preferred_element_type=F32)
    t = z[:, 0:1]
    o_ref[...] = h * jnp.exp(-t * d_ref[...])


def _hyena_filters(seq_len, w1, b1, w2, b2, w3, b3, w4, freq, tl):
    fw = w1.shape[1]
    c2 = w4.shape[1]
    c = c2 // 2
    t = jnp.linspace(0.0, 1.0, seq_len, dtype=F32)[:, None]
    w = 2.0 * math.pi * jnp.arange(seq_len, dtype=F32)[:, None] / seq_len
    f = jnp.linspace(1e-4, HYENA_BANDS - 1, HYENA_BANDS, dtype=F32)[None]
    z = jnp.concatenate([t, jnp.cos(f * w), -jnp.sin(f * w)], -1)
    emb = z.shape[1]
    z = jnp.pad(z, ((0, 0), (0, LANES - emb)))
    w1p = jnp.pad(w1, ((0, LANES - emb), (0, 0)))
    max_decay = math.log(DECAY_TARGET) / FAST_DECAY_PCT
    min_decay = math.log(DECAY_TARGET) / SLOW_DECAY_PCT
    deltas = jnp.abs(jnp.linspace(min_decay, max_decay, c, dtype=F32))
    deltas2 = jnp.concatenate([deltas, deltas]).reshape(1, c2)
    full = lambda a: pl.BlockSpec(a.shape, lambda i: (0,) * a.ndim)
    ops = [w1p, b1.reshape(1, fw), w2, b2.reshape(1, fw), w3, b3.reshape(1, fw), w4,
           freq.reshape(1, fw), deltas2]
    return pl.pallas_call(
        _filter_kernel,
        grid=(seq_len // tl,),
        in_specs=[pl.BlockSpec((tl, LANES), lambda i: (i, 0))] + [full(a) for a in ops],
        out_specs=pl.BlockSpec((tl, c2), lambda i: (i, 0)),
        out_shape=jax.ShapeDtypeStruct((seq_len, c2), F32),
        compiler_params=_cparams("parallel"),
        name="hyena_filters",
    )(z, *ops)


def _left_matmul_kernel(w_ref, x_ref, o_ref):
    o_ref[...] = jnp.dot(w_ref[...], x_ref[...], preferred_element_type=F32).astype(o_ref.dtype)


def _left_matmul(w, x, tn, out_dtype):
    g, k, n = x.shape
    m = w.shape[0]
    return pl.pallas_call(
        _left_matmul_kernel,
        grid=(g, n // tn),
        in_specs=[pl.BlockSpec((m, k), lambda a, j: (0, 0)),
                  pl.BlockSpec((None, k, tn), lambda a, j: (a, 0, j))],
        out_specs=pl.BlockSpec((None, m, tn), lambda a, j: (a, 0, j)),
        out_shape=jax.ShapeDtypeStruct((g, m, n), out_dtype),
        compiler_params=_cparams("parallel", "parallel"),
        name="dft_outer",
    )(w, x)


def _cmul_tiled(ar, ai, tr, ti, out_ref, n2):
    for cb in range(ar.shape[1] // LANES):
        sl = slice(cb * LANES, (cb + 1) * LANES)
        xr = ar[:, sl]
        xi = ai[:, sl]
        out_ref[0:n2, sl] = (xr * tr - xi * ti).astype(out_ref.dtype)
        out_ref[n2:2 * n2, sl] = (xr * ti + xi * tr).astype(out_ref.dtype)


def _spectrum_kernel(a_ref, tr_ref, ti_ref, wf_ref, o_ref, s_ref, *, scale):
    n2 = tr_ref.shape[0]
    a = a_ref[...].astype(F32)
    _cmul_tiled(a[:n2], a[n2:], tr_ref[...], ti_ref[...], s_ref, n2)
    o_ref[...] = jnp.dot(wf_ref[...], s_ref[...], preferred_element_type=F32) * scale


def _filter_spectrum(a, tw_r, tw_i, wf, scale):
    n1, r, c = a.shape
    n2 = r // 2
    kern = functools.partial(_spectrum_kernel, scale=scale)
    return pl.pallas_call(
        kern,
        grid=(n1,),
        in_specs=[pl.BlockSpec((None, r, c), lambda k: (k, 0, 0)),
                  pl.BlockSpec((None, n2, LANES), lambda k: (k, 0, 0)),
                  pl.BlockSpec((None, n2, LANES), lambda k: (k, 0, 0)),
                  pl.BlockSpec((r, r), lambda k: (0, 0))],
        out_specs=pl.BlockSpec((None, r, c), lambda k: (k, 0, 0)),
        out_shape=jax.ShapeDtypeStruct((n1, r, c), F32),
        scratch_shapes=[pltpu.VMEM((r, c), BF16)],
        compiler_params=_cparams("parallel"),
        name="filter_spectrum",
    )(a, tw_r, tw_i, wf)


def _fft_mid_kernel(a_ref, h_ref, tr_ref, ti_ref, wf_ref, wi_ref, o_ref, s_ref):
    n2 = tr_ref.shape[0]
    tr = tr_ref[...]
    ti = ti_ref[...]
    a = a_ref[...].astype(F32)
    _cmul_tiled(a[:n2], a[n2:], tr, ti, s_ref, n2)
    x = jnp.dot(wf_ref[...], s_ref[...], preferred_element_type=F32)
    h = h_ref[...]
    xr, xi = x[:n2], x[n2:]
    hr, hi = h[:n2], h[n2:]
    s_ref[0:n2, :] = (xr * hr - xi * hi).astype(s_ref.dtype)
    s_ref[n2:2 * n2, :] = (xr * hi + xi * hr).astype(s_ref.dtype)
    z = jnp.dot(wi_ref[...], s_ref[...], preferred_element_type=F32)
    _cmul_tiled(z[:n2], z[n2:], tr, -ti, o_ref, n2)


def _fft_mid(a, hspec, tw_r, tw_i, wf, wi, batch):
    g, r, c = a.shape
    n1 = g // batch
    n2 = r // 2
    return pl.pallas_call(
        _fft_mid_kernel,
        grid=(n1, batch),
        in_specs=[pl.BlockSpec((None, r, c), lambda k, b: (b * n1 + k, 0, 0)),
                  pl.BlockSpec((None, r, c), lambda k, b: (k, 0, 0)),
                  pl.BlockSpec((None, n2, LANES), lambda k, b: (k, 0, 0)),
                  pl.BlockSpec((None, n2, LANES), lambda k, b: (k, 0, 0)),
                  pl.BlockSpec((r, r), lambda k, b: (0, 0)),
                  pl.BlockSpec((r, r), lambda k, b: (0, 0))],
        out_specs=pl.BlockSpec((None, r, c), lambda k, b: (b * n1 + k, 0, 0)),
        out_shape=jax.ShapeDtypeStruct((g, r, c), BF16),
        scratch_shapes=[pltpu.VMEM((r, c), BF16)],
        compiler_params=_cparams("parallel", "arbitrary"),
        name="fft_mid",
    )(a, hspec, tw_r, tw_i, wf, wi)


def _fft_out_kernel(w_ref, z_ref, u_ref, x0_ref, b_ref, o_ref):
    y = jnp.dot(w_ref[...], z_ref[...], preferred_element_type=F32)
    u = u_ref[...].astype(F32)
    o_ref[...] = ((y + u * b_ref[...]) * x0_ref[...].astype(F32)).astype(o_ref.dtype)


def _fft_out(w, z, u2, x02, bias_t, tn):
    g, k, n = z.shape
    m = w.shape[0]
    return pl.pallas_call(
        _fft_out_kernel,
        grid=(g, n // tn),
        in_specs=[pl.BlockSpec((m, k), lambda a, j: (0, 0)),
                  pl.BlockSpec((None, k, tn), lambda a, j: (a, 0, j)),
                  pl.BlockSpec((None, m, tn), lambda a, j: (a, 0, j)),
                  pl.BlockSpec((None, m, tn), lambda a, j: (a, 0, j)),
                  pl.BlockSpec((1, tn), lambda a, j: (0, 0))],
        out_specs=pl.BlockSpec((None, m, tn), lambda a, j: (a, 0, j)),
        out_shape=jax.ShapeDtypeStruct((g, m, n), BF16),
        compiler_params=_cparams("parallel", "parallel"),
        name="fft_out",
    )(w, z, u2, x02, bias_t)


def _dft_tables(seq_len):
    n = 2 * seq_len
    n2 = DFT_INNER
    n1 = n // n2
    n1h = n1 // 2
    k1 = np.arange(n1)[:, None]
    m1 = np.arange(n1)[None, :]
    ang1 = 2.0 * np.pi * k1 * m1 / n1
    w1_full = np.concatenate([np.cos(ang1), -np.sin(ang1)], axis=0)
    w1_half = w1_full[:, :n1h]
    ang1i = ang1.T[:n1h]
    w1_inv = np.concatenate([np.cos(ang1i), -np.sin(ang1i)], axis=1)
    k2 = np.arange(n2)[:, None]
    m2 = np.arange(n2)[None, :]
    ang2 = 2.0 * np.pi * k2 * m2 / n2
    c, s = np.cos(ang2), np.sin(ang2)
    w2_fwd = np.block([[c, s], [-s, c]])
    w2_inv = np.block([[c, -s], [s, c]])
    angt = 2.0 * np.pi * np.arange(n1)[:, None] * np.arange(n2)[None, :] / n
    tw_r = np.repeat(np.cos(angt)[:, :, None], LANES, axis=2)
    tw_i = np.repeat(-np.sin(angt)[:, :, None], LANES, axis=2)
    bf = lambda a: jnp.asarray(a, F32).astype(BF16)
    return dict(n1=n1, n2=n2, n1h=n1h, w1_full=bf(w1_full), w1_half=bf(w1_half), w1_inv=bf(w1_inv),
                w2_fwd=bf(w2_fwd), w2_inv=bf(w2_inv),
                tw_r=jnp.asarray(tw_r, F32), tw_i=jnp.asarray(tw_i, F32))


def _to_mid_layout(a, groups, n1, n2, c):
    a = a.reshape(groups, 2, n1, n2, c)
    a = jnp.transpose(a, (0, 2, 1, 3, 4))
    return a.reshape(groups * n1, 2 * n2, c)


def _from_mid_layout(z, groups, n1, n2, c):
    z = z.reshape(groups, n1, 2, n2, c)
    z = jnp.transpose(z, (0, 2, 1, 3, 4))
    return z.reshape(groups, 2 * n1, n2 * c)


def _hyena_long_conv(u, x0, filt, bias, batch, seq_len, tab):
    c = u.shape[1]
    n1, n2, n1h = tab["n1"], tab["n2"], tab["n1h"]
    n = n1 * n2
    tn = _pick(n2 * c, 2048)
    h_fwd, h_bwd = filt[:, :c], filt[:, c:]
    circ = jnp.concatenate([h_fwd, jnp.zeros((1, c), F32), h_bwd[1:][::-1]], axis=0).astype(BF16)
    fa = _left_matmul(tab["w1_full"], circ.reshape(1, n1, n2 * c), tn, BF16)
    hspec = _filter_spectrum(_to_mid_layout(fa, 1, n1, n2, c), tab["tw_r"], tab["tw_i"],
                             tab["w2_fwd"], 1.0 / n)
    u2 = u.reshape(batch, n1h, n2 * c)
    x02 = x0.reshape(batch, n1h, n2 * c)
    a = _left_matmul(tab["w1_half"], u2, tn, BF16)
    z = _fft_mid(_to_mid_layout(a, batch, n1, n2, c), hspec, tab["tw_r"], tab["tw_i"],
                 tab["w2_fwd"], tab["w2_inv"], batch)
    z2 = _from_mid_layout(z, batch, n1, n2, c)
    bias_t = jnp.tile(bias.reshape(1, c), (1, tn // c))
    y = _fft_out(tab["w1_inv"], z2, u2, x02, bias_t, tn)
    return y.reshape(batch * seq_len, c)


def _merge_kernel(attn_ref, hy_ref, gate_ref, x_ref, wa_ref, wh_ref, wo_ref, g2_ref,
                  x1_ref, hn_ref):
    d = x_ref.shape[1]
    pa = jnp.dot(attn_ref[...], wa_ref[...], preferred_element_type=F32)
    ph = jnp.dot(hy_ref[...], wh_ref[...], preferred_element_type=F32)
    merged = gate_ref[:, :d].astype(F32) * pa + gate_ref[:, d:].astype(F32) * ph
    x1 = x_ref[...] + jnp.dot(merged.astype(BF16), wo_ref[...], preferred_element_type=F32)
    x1_ref[...] = x1
    ms = jnp.mean(x1 * x1, axis=-1, keepdims=True)
    hn_ref[...] = (x1 * lax.rsqrt(ms + NORM_EPS) * g2_ref[...]).astype(hn_ref.dtype)


def _merge(attn, hyena, gates, x, wa, wh, wo, norm2, tm):
    t, d = x.shape
    const = lambda a: pl.BlockSpec(a.shape, lambda i: (0, 0), pipeline_mode=pl.Buffered(1))
    row = lambda n: pl.BlockSpec((tm, n), lambda i: (i, 0))
    g2 = norm2.reshape(1, d)
    return pl.pallas_call(
        _merge_kernel,
        grid=(t // tm,),
        in_specs=[row(attn.shape[1]), row(hyena.shape[1]), row(gates.shape[1]), row(d),
                  const(wa), const(wh), const(wo), const(g2)],
        out_specs=[row(d), row(d)],
        out_shape=[jax.ShapeDtypeStruct((t, d), F32), jax.ShapeDtypeStruct((t, d), BF16)],
        compiler_params=_cparams("parallel"),
        name="merge_out_proj",
    )(attn, hyena, gates, x, wa, wh, wo, g2)


def _ffn_kernel(prev_ref, main_ref, next_ref, x1_ref, wg_ref, wv_ref, cg_ref, cv_ref, bg_ref, bv_ref,
                wd_ref, gf_ref, o_ref, ext_ref, acc_ref, *, blocks_per_seq):
    i = pl.program_id(0)
    j = pl.program_id(1)
    halo = prev_ref.shape[0]
    tm = main_ref.shape[0]

    @pl.when(j == 0)
    def _():
        first = (i % blocks_per_seq) == 0
        last = (i % blocks_per_seq) == blocks_per_seq - 1
        zero = jnp.zeros(prev_ref.shape, prev_ref.dtype)
        ext_ref[0:halo, :] = jnp.where(first, zero, prev_ref[...])
        ext_ref[halo:halo + tm, :] = main_ref[...]
        ext_ref[halo + tm:, :] = jnp.where(last, zero, next_ref[...])
        acc_ref[...] = jnp.zeros(acc_ref.shape, F32)

    ext = ext_ref[...]
    ug = jnp.dot(ext, wg_ref[...], preferred_element_type=F32)
    uv = jnp.dot(ext, wv_ref[...], preferred_element_type=F32)
    ug = _conv3_rows(ug, cg_ref[...], tm, halo) + bg_ref[...]
    uv = _conv3_rows(uv, cv_ref[...], tm, halo) + bv_ref[...]
    act = (ug / (1.0 + jnp.exp(-ug))) * uv
    acc_ref[...] += jnp.dot(act.astype(BF16), wd_ref[...], preferred_element_type=F32)

    @pl.when(j == pl.num_programs(1) - 1)
    def _():
        y = x1_ref[...] + acc_ref[...]
        ms = jnp.mean(y * y, axis=-1, keepdims=True)
        o_ref[...] = y * lax.rsqrt(ms + NORM_EPS) * gf_ref[...]


def _ffn(hn2, x1, w_up, conv_w, conv_b, w_down, norm_f, seq_len, tm, tf):
    t, d = x1.shape
    dff = w_down.shape[0]
    nj = dff // tf
    halo = BF16_SUBLANES
    r = tm // halo
    nb = t // halo
    kern = functools.partial(_ffn_kernel, blocks_per_seq=seq_len // tm)
    cb = conv_b.reshape(1, 2 * dff)
    return pl.pallas_call(
        kern,
        grid=(t // tm, nj),
        in_specs=[pl.BlockSpec((halo, d), lambda i, j: (jnp.maximum(i * r - 1, 0), 0)),
                  pl.BlockSpec((tm, d), lambda i, j: (i, 0)),
                  pl.BlockSpec((halo, d), lambda i, j: (jnp.minimum((i + 1) * r, nb - 1), 0)),
                  pl.BlockSpec((tm, d), lambda i, j: (i, 0)),
                  pl.BlockSpec((d, tf), lambda i, j: (0, j)),
                  pl.BlockSpec((d, tf), lambda i, j: (0, nj + j)),
                  pl.BlockSpec((3, tf), lambda i, j: (0, j)),
                  pl.BlockSpec((3, tf), lambda i, j: (0, nj + j)),
                  pl.BlockSpec((1, tf), lambda i, j: (0, j)),
                  pl.BlockSpec((1, tf), lambda i, j: (0, nj + j)),
                  pl.BlockSpec((tf, d), lambda i, j: (j, 0)),
                  pl.BlockSpec((1, d), lambda i, j: (0, 0))],
        out_specs=pl.BlockSpec((tm, d), lambda i, j: (i, 0)),
        out_shape=jax.ShapeDtypeStruct((t, d), F32),
        scratch_shapes=[pltpu.VMEM((tm + 2 * halo, d), BF16), pltpu.VMEM((tm, d), F32)],
        compiler_params=_cparams("parallel", "arbitrary"),
        name="conv_glu_mlp",
    )(hn2, hn2, hn2, x1, w_up, w_up, conv_w, conv_w, cb, cb, w_down, norm_f.reshape(1, d))


def _tiles(seq_len, d_model, d_ff):
    return dict(
        tm_norm=_pick(seq_len, 512),
        tm_proj=_pick(seq_len, 1024),
        tn_proj=1024,
        tq=_pick(seq_len, 512),
        tk=_pick(seq_len, 512),
        tm_prep=_pick(seq_len, 512),
        tl_filt=_pick(seq_len, 512),
        tm_merge=_pick(seq_len, 256),
        tm_ffn=_pick(seq_len, 512),
        tf_ffn=_pick(d_ff, 512),
    )


def _encoder_layer(x, batch, seq_len, lambda_init, p, consts):
    t, d = x.shape
    tl = _tiles(seq_len, d, p["w_down"].shape[0])
    hn = _rmsnorm(x, p["norm1"], tl["tm_norm"])

    qk = _proj_rope(hn, p["w_qk"], consts["cos"], consts["sin"], seq_len, tl["tm_proj"])
    v = _proj_plain(hn, p["w_v"], tl["tm_proj"], _pick(p["w_v"].shape[1], tl["tn_proj"]))
    hy = _proj_plain(hn, p["w_hy"], tl["tm_proj"], _pick(p["w_hy"].shape[1], tl["tn_proj"]))
    gates = _proj_gate(hn, p["w_gl"], p["gate_b"], tl["tm_proj"], _pick(p["w_gl"].shape[1], tl["tn_proj"]))

    attn = _attention(qk.reshape(batch, seq_len, -1), v.reshape(batch, seq_len, -1),
                      p["lambda_q1"], p["lambda_k1"], p["lambda_q2"], p["lambda_k2"], p["subln_g"],
                      batch, seq_len, lambda_init, tl["tq"], tl["tk"]).reshape(t, ATTN_WIDTH)

    u, x0 = _hyena_prep(hy, p["in_conv_w"], p["in_conv_b"], seq_len, tl["tm_prep"])
    filt = _hyena_filters(seq_len, p["filt_w1"], p["filt_b1"], p["filt_w2"], p["filt_b2"],
                          p["filt_w3"], p["filt_b3"], p["filt_w4"], p["filt_freq"], tl["tl_filt"])
    hyena = _hyena_long_conv(u, x0, filt, p["hyena_bias"], batch, seq_len, consts["dft"])

    x1, hn2 = _merge(attn, hyena, gates, x, p["w_attn_out"], p["w_hyena_out"], p["w_out"], p["norm2"],
                     tl["tm_merge"])
    return x1, hn2, tl


def kernel(x_prompt, x_sample, norm1, w_in, in_conv_w, in_conv_b, gate_b, lambda_q1, lambda_k1, lambda_q2, lambda_k2, subln_g, filt_w1, filt_b1, filt_w2, filt_b2, filt_w3, filt_b3, filt_w4, filt_freq, hyena_bias, w_attn_out, w_hyena_out, w_out, norm2, w_up, ffn_conv_w, ffn_conv_b, w_down, norm_f):
    depth = norm1.shape[0]
    assert depth == 1, "this kernel implements the single-layer trunk"
    d_model = x_prompt.shape[-1]
    hyena_width = hyena_bias.shape[-1]
    o1 = ATTN_QK_WIDTH
    o2 = o1 + ATTN_QK_WIDTH
    o3 = o2 + ATTN_WIDTH
    o4 = o3 + 3 * hyena_width

    layers = []
    for i in range(depth):
        wi = w_in[i].astype(BF16)
        layers.append(dict(
            norm1=norm1[i], w_qk=wi[:, :o2], w_v=wi[:, o2:o3], w_hy=wi[:, o3:o4], w_gl=wi[:, o4:],
            in_conv_w=in_conv_w[i], in_conv_b=in_conv_b[i], gate_b=gate_b[i],
            lambda_q1=lambda_q1[i], lambda_k1=lambda_k1[i], lambda_q2=lambda_q2[i], lambda_k2=lambda_k2[i],
            subln_g=subln_g[i],
            filt_w1=filt_w1[i], filt_b1=filt_b1[i], filt_w2=filt_w2[i], filt_b2=filt_b2[i],
            filt_w3=filt_w3[i], filt_b3=filt_b3[i], filt_w4=filt_w4[i], filt_freq=filt_freq[i],
            hyena_bias=hyena_bias[i],
            w_attn_out=w_attn_out[i].astype(BF16), w_hyena_out=w_hyena_out[i].astype(BF16),
            w_out=w_out[i].astype(BF16), norm2=norm2[i],
            w_up=w_up[i].astype(BF16), ffn_conv_w=ffn_conv_w[i], ffn_conv_b=ffn_conv_b[i],
            w_down=w_down[i].astype(BF16)))

    def trunk(x3):
        batch, seq_len, _ = x3.shape
        cos, sin = _rope_tables(seq_len)
        consts = dict(cos=cos, sin=sin, dft=_dft_tables(seq_len))
        x = x3.reshape(batch * seq_len, d_model)
        lambda_init = 0.8 - 0.6 * math.exp(-0.3 * 0)
        p = layers[0]
        x1, hn2, tl = _encoder_layer(x, batch, seq_len, lambda_init, p, consts)
        y = _ffn(hn2, x1, p["w_up"], p["ffn_conv_w"], p["ffn_conv_b"], p["w_down"], norm_f, seq_len,
                 tl["tm_ffn"], tl["tf_ffn"])
        return y.reshape(batch, seq_len, d_model)

    return (trunk(x_prompt), trunk(x_sample))
```

```python
import functools
import math

import numpy as np
import jax
import jax.numpy as jnp
from jax import lax
from jax.experimental import pallas as pl
from jax.experimental.pallas import tpu as pltpu

ATTN_HEADS = 8
ATTN_QK_DIM = 64
ATTN_V_DIM = 2 * ATTN_QK_DIM
ATTN_QK_WIDTH = ATTN_HEADS * 2 * ATTN_QK_DIM
ATTN_WIDTH = ATTN_HEADS * ATTN_V_DIM
HYENA_EMB_DIM = 33
HYENA_BANDS = (HYENA_EMB_DIM - 1) // 2
DECAY_TARGET = 1e-2
FAST_DECAY_PCT = 0.3
SLOW_DECAY_PCT = 1.5
ROPE_THETA = 10000.0
NORM_EPS = 1e-6

LANES = 128
BF16_SUBLANES = 16
VMEM_LIMIT_BYTES = 56 * 1024 * 1024

DFT_INNER = LANES

F32 = jnp.float32
BF16 = jnp.bfloat16


def _cparams(*sem):
    return pltpu.CompilerParams(dimension_semantics=sem, vmem_limit_bytes=VMEM_LIMIT_BYTES)


def _pick(n, pref):
    t = min(n, pref)
    while n % t:
        t //= 2
    return t


def _rmsnorm_kernel(x_ref, g_ref, o_ref):
    x = x_ref[...]
    ms = jnp.mean(x * x, axis=-1, keepdims=True)
    o_ref[...] = (x * lax.rsqrt(ms + NORM_EPS) * g_ref[...]).astype(o_ref.dtype)


def _rmsnorm(x, g, tm):
    t, d = x.shape
    return pl.pallas_call(
        _rmsnorm_kernel,
        grid=(t // tm,),
        in_specs=[pl.BlockSpec((tm, d), lambda i: (i, 0)),
                  pl.BlockSpec((1, d), lambda i: (0, 0))],
        out_specs=pl.BlockSpec((tm, d), lambda i: (i, 0)),
        out_shape=jax.ShapeDtypeStruct((t, d), BF16),
        compiler_params=_cparams("parallel"),
        name="rmsnorm_in",
    )(x, g.reshape(1, d))


def _proj_plain_kernel(h_ref, w_ref, o_ref):
    o_ref[...] = jnp.dot(h_ref[...], w_ref[...], preferred_element_type=F32).astype(o_ref.dtype)


def _proj_plain(h, w, tm, tn):
    t, d = h.shape
    n = w.shape[1]
    return pl.pallas_call(
        _proj_plain_kernel,
        grid=(t // tm, n // tn),
        in_specs=[pl.BlockSpec((tm, d), lambda i, j: (i, 0)),
                  pl.BlockSpec((d, tn), lambda i, j: (0, j))],
        out_specs=pl.BlockSpec((tm, tn), lambda i, j: (i, j)),
        out_shape=jax.ShapeDtypeStruct((t, n), BF16),
        compiler_params=_cparams("parallel", "arbitrary"),
        name="proj_plain",
    )(h, w)


def _proj_gate_kernel(h_ref, w_ref, b_ref, o_ref):
    z = jnp.dot(h_ref[...], w_ref[...], preferred_element_type=F32) + b_ref[...]
    o_ref[...] = (1.0 / (1.0 + jnp.exp(-z))).astype(o_ref.dtype)


def _proj_gate(h, w, b, tm, tn):
    t, d = h.shape
    n = w.shape[1]
    return pl.pallas_call(
        _proj_gate_kernel,
        grid=(t // tm, n // tn),
        in_specs=[pl.BlockSpec((tm, d), lambda i, j: (i, 0)),
                  pl.BlockSpec((d, tn), lambda i, j: (0, j)),
                  pl.BlockSpec((1, tn), lambda i, j: (0, j))],
        out_specs=pl.BlockSpec((tm, tn), lambda i, j: (i, j)),
        out_shape=jax.ShapeDtypeStruct((t, n), BF16),
        compiler_params=_cparams("parallel", "arbitrary"),
        name="proj_gate",
    )(h, w, b.reshape(1, n))


def _proj_rope_kernel(h_ref, w_ref, cos_ref, sin_ref, o_ref, *, q_scale):
    acc = jnp.dot(h_ref[...], w_ref[...], preferred_element_type=F32)
    cos = cos_ref[...]
    sin = sin_ref[...]
    lane = lax.broadcasted_iota(jnp.int32, cos.shape, 1)
    first_half = (lane % ATTN_QK_DIM) < (ATTN_QK_DIM // 2)
    scale = jnp.where(pl.program_id(1) == 0, q_scale, 1.0).astype(F32)
    half = ATTN_QK_DIM // 2
    for g in range(acc.shape[1] // LANES):
        x = acc[:, g * LANES:(g + 1) * LANES]
        up = pltpu.roll(x, LANES - half, 1)
        down = pltpu.roll(x, half, 1)
        rot = jnp.where(first_half, up, down)
        o_ref[:, g * LANES:(g + 1) * LANES] = ((x * cos + rot * sin) * scale).astype(o_ref.dtype)


def _proj_rope(h, w, cos, sin, seq_len, tm):
    t, d = h.shape
    n = w.shape[1]
    tn = ATTN_QK_WIDTH
    nblk = seq_len // tm
    kern = functools.partial(_proj_rope_kernel, q_scale=ATTN_QK_DIM ** -0.5 * math.log2(math.e))
    return pl.pallas_call(
        kern,
        grid=(t // tm, n // tn),
        in_specs=[pl.BlockSpec((tm, d), lambda i, j: (i, 0)),
                  pl.BlockSpec((d, tn), lambda i, j: (0, j)),
                  pl.BlockSpec((tm, LANES), lambda i, j: (i % nblk, 0)),
                  pl.BlockSpec((tm, LANES), lambda i, j: (i % nblk, 0))],
        out_specs=pl.BlockSpec((tm, tn), lambda i, j: (i, j)),
        out_shape=jax.ShapeDtypeStruct((t, n), BF16),
        compiler_params=_cparams("parallel", "arbitrary"),
        name="proj_rope",
    )(h, w, cos, sin)


def _rope_tables(seq_len):
    d = ATTN_QK_DIM
    inv = ROPE_THETA ** (-jnp.arange(0, d, 2, dtype=F32) / d)
    ang = jnp.arange(seq_len, dtype=F32)[:, None] * inv[None]
    ang = jnp.concatenate([ang, ang], -1)
    sign = jnp.concatenate([-jnp.ones((d // 2,), F32), jnp.ones((d // 2,), F32)])
    cos = jnp.cos(ang)
    sin = jnp.sin(ang) * sign
    reps = LANES // d
    return jnp.tile(cos, (1, reps)), jnp.tile(sin, (1, reps))


def _attn_kernel(q_ref, k_ref, v_ref, lq1_ref, lk1_ref, lq2_ref, lk2_ref, g_ref, o_ref,
                 sa0_ref, sb0_ref, sa1_ref, sb1_ref, m0_ref, m1_ref, acc0_ref, acc1_ref,
                 *, tk, lambda_init):
    seq = k_ref.shape[0]
    n_chunks = seq // tk
    q = q_ref[...]
    lane = lax.broadcasted_iota(jnp.int32, q.shape, 1)
    zero = jnp.zeros_like(q)
    qa = jnp.where(lane < ATTN_QK_DIM, q, zero)
    qb = jnp.where(lane >= ATTN_QK_DIM, q, zero)

    m0_ref[...] = jnp.full(m0_ref.shape, -jnp.inf, F32)
    m1_ref[...] = jnp.full(m1_ref.shape, -jnp.inf, F32)
    acc0_ref[...] = jnp.zeros(acc0_ref.shape, F32)
    acc1_ref[...] = jnp.zeros(acc1_ref.shape, F32)

    ones_col = (lax.broadcasted_iota(jnp.int32, (tk, LANES), 1) == 0).astype(BF16)
    nt = (((1,), (1,)), ((), ()))
    score_refs = ((sa0_ref, sb0_ref), (sa1_ref, sb1_ref))

    def scores(j, slot):
        start = j * tk if isinstance(j, int) else pl.multiple_of(j * tk, tk)
        kc = k_ref[pl.ds(start, tk), :]
        sa_ref, sb_ref = score_refs[slot]
        sa_ref[...] = lax.dot_general(qa, kc, nt, preferred_element_type=F32)
        sb_ref[...] = lax.dot_general(qb, kc, nt, preferred_element_type=F32)

    def one_map(s_ref, vext, m_ref, acc_ref):
        s = s_ref[...]
        m_old = m_ref[...]
        m_new = jnp.maximum(m_old, jnp.max(s, axis=1, keepdims=True))
        alpha = jnp.exp2(m_old - m_new)
        p = jnp.concatenate(
            [jnp.exp2(s[:, c * LANES:(c + 1) * LANES] - m_new) for c in range(tk // LANES)],
            axis=1).astype(BF16)
        pv = jnp.dot(p, vext, preferred_element_type=F32)
        acc_ref[...] = jnp.concatenate([alpha, alpha], axis=1) * acc_ref[...] + pv
        m_ref[...] = m_new

    def weighted_values(j, slot):
        start = j * tk if isinstance(j, int) else pl.multiple_of(j * tk, tk)
        vext = jnp.concatenate([v_ref[pl.ds(start, tk), :], ones_col], axis=1)
        sa_ref, sb_ref = score_refs[slot]
        one_map(sa_ref, vext, m0_ref, acc0_ref)
        one_map(sb_ref, vext, m1_ref, acc1_ref)

    def chunk_pair(j, is_last):
        scores(j + 1, 1)
        weighted_values(j, 0)
        if not is_last:
            scores(j + 2, 0)
        weighted_values(j + 1, 1)

    scores(0, 0)

    def body(i, carry):
        chunk_pair(2 * i, False)
        return carry

    lax.fori_loop(0, n_chunks // 2 - 1, body, 0)
    chunk_pair(n_chunks - 2, True)

    lam =(jnp.exp(jnp.sum(lq1_ref[...] * lk1_ref[...], axis=1, keepdims=True))
           - jnp.exp(jnp.sum(lq2_ref[...] * lk2_ref[...], axis=1, keepdims=True))
           + lambda_init)
    a0 = acc0_ref[...]
    a1 = acc1_ref[...]
    o0 = a0[:, :LANES] / a0[:, LANES:LANES + 1]
    o1 = a1[:, :LANES] / a1[:, LANES:LANES + 1]
    o = o0 - lam * o1
    ms = jnp.mean(o * o, axis=-1, keepdims=True)
    o = o * lax.rsqrt(ms + NORM_EPS) * g_ref[...]
    o_ref[...] = (o * (1.0 - lambda_init)).astype(o_ref.dtype)


def _attention(qk, v, lq1, lk1, lq2, lk2, subln_g, batch, seq_len, lambda_init, tq, tk):
    assert (seq_len // tk) % 2 == 0, "key chunks are processed in pairs"
    kern = functools.partial(_attn_kernel, tk=tk, lambda_init=lambda_init)
    small =pl.BlockSpec((1, ATTN_QK_DIM), lambda b, h, i: (0, 0))
    return pl.pallas_call(
        kern,
        grid=(batch, ATTN_HEADS, seq_len // tq),
        in_specs=[pl.BlockSpec((None, tq, LANES), lambda b, h, i: (b, i, h)),
                  pl.BlockSpec((None, seq_len, LANES), lambda b, h, i: (b, 0, ATTN_HEADS + h)),
                  pl.BlockSpec((None, seq_len, LANES), lambda b, h, i: (b, 0, h)),
                  small, small, small, small,
                  pl.BlockSpec((1, LANES), lambda b, h, i: (0, 0))],
        out_specs=pl.BlockSpec((None, tq, LANES), lambda b, h, i: (b, i, h)),
        out_shape=jax.ShapeDtypeStruct((batch, seq_len, ATTN_WIDTH), BF16),
        scratch_shapes=[pltpu.VMEM((tq, tk), F32)] * 4
                       + [pltpu.VMEM((tq, LANES), F32), pltpu.VMEM((tq, LANES), F32),
                          pltpu.VMEM((tq, 2 * LANES), F32), pltpu.VMEM((tq, 2 * LANES), F32)],
        compiler_params=_cparams("parallel", "parallel", "arbitrary"),
        name="diff_attention",
    )(qk, qk, v, lq1.reshape(1, -1), lk1.reshape(1, -1), lq2.reshape(1, -1), lk2.reshape(1, -1),
      subln_g.reshape(1, -1))


def _conv3_rows(e, w, n_rows, halo):
    tot = e.shape[0]
    prev = pltpu.roll(e, 1, 0)
    nxt = pltpu.roll(e, tot - 1, 0)
    c = prev * w[0:1, :] + e * w[1:2, :] + nxt * w[2:3, :]
    return c[halo:halo + n_rows, :]


def _hyena_prep_kernel(prev_ref, main_ref, next_ref, w_ref, b_ref, u_ref, x0_ref, *, blocks_per_seq):
    i = pl.program_id(0)
    first = (i % blocks_per_seq) == 0
    last = (i % blocks_per_seq) == blocks_per_seq - 1
    halo = prev_ref.shape[0]
    tm = main_ref.shape[0]
    prev = jnp.where(first, 0.0, prev_ref[...].astype(F32))
    nxt = jnp.where(last, 0.0, next_ref[...].astype(F32))
    e = jnp.concatenate([prev, main_ref[...].astype(F32), nxt], axis=0)
    c = _conv3_rows(e, w_ref[...], tm, halo) + b_ref[...]
    width = u_ref.shape[1]
    x0 = c[:, :width]
    x1 = c[:, width:2 * width]
    hv = c[:, 2 * width:]
    u_ref[...] = (hv * x1).astype(u_ref.dtype)
    x0_ref[...] = x0.astype(x0_ref.dtype)


def _hyena_prep(hy, conv_w, conv_b, seq_len, tm):
    t, n = hy.shape
    width = n // 3
    halo = BF16_SUBLANES
    r = tm // halo
    nb = t // halo
    kern = functools.partial(_hyena_prep_kernel, blocks_per_seq=seq_len // tm)
    return pl.pallas_call(
        kern,
        grid=(t // tm,),
        in_specs=[pl.BlockSpec((halo, n), lambda i: (jnp.maximum(i * r - 1, 0), 0)),
                  pl.BlockSpec((tm, n), lambda i: (i, 0)),
                  pl.BlockSpec((halo, n), lambda i: (jnp.minimum((i + 1) * r, nb - 1), 0)),
                  pl.BlockSpec((3, n), lambda i: (0, 0)),
                  pl.BlockSpec((1, n), lambda i: (0, 0))],
        out_specs=[pl.BlockSpec((tm, width), lambda i: (i, 0)),
                   pl.BlockSpec((tm, width), lambda i: (i, 0))],
        out_shape=[jax.ShapeDtypeStruct((t, width), BF16),
                   jax.ShapeDtypeStruct((t, width), BF16)],
        compiler_params=_cparams("parallel"),
        name="hyena_prep",
    )(hy, hy, hy, conv_w, conv_b.reshape(1, n))


def _filter_kernel(z_ref, w1_ref, b1_ref, w2_ref, b2_ref, w3_ref, b3_ref, w4_ref, f_ref, d_ref, o_ref,
                   *, seq_len):
    hp = lax.Precision.HIGHEST
    z = z_ref[...]
    f = f_ref[...]
    h = jnp.sin(f * (jnp.dot(z, w1_ref[...], precision=hp, preferred_element_type=F32) + b1_ref[...]))
    h = jnp.sin(f * (jnp.dot(h, w2_ref[...], precision=hp, preferred_element_type=F32) + b2_ref[...]))
    h = jnp.sin(f * (jnp.dot(h, w3_ref[...], precision=hp, preferred_element_type=F32) + b3_ref[...]))
    h = jnp.dot(h, w4_ref[...], precision=hp, preferred_element_type=F32)
    t = z[:, 0:1]
    h = h * jnp.exp(-t * d_ref[...])
    row = pl.program_id(0) * z.shape[0] + lax.broadcasted_iota(jnp.int32, (z.shape[0], 1), 0)
    o_ref[...] = jnp.where(row == seq_len, 0.0, h).astype(o_ref.dtype)


def _hyena_filters(seq_len, w1, b1, w2, b2, w3, b3, w4, freq, tl):
    fw = w1.shape[1]
    c = w4.shape[1] // 2
    t = jnp.linspace(0.0, 1.0, seq_len, dtype=F32)[:, None]
    w = 2.0 * math.pi * jnp.arange(seq_len, dtype=F32)[:, None] / seq_len
    f = jnp.linspace(1e-4, HYENA_BANDS - 1, HYENA_BANDS, dtype=F32)[None]
    z = jnp.concatenate([t, jnp.cos(f * w), -jnp.sin(f * w)], -1)
    emb = z.shape[1]
    z = jnp.concatenate([z, jnp.zeros((1, emb), F32), z[1:][::-1]], axis=0)
    z = jnp.pad(z, ((0, 0), (0, LANES - emb)))
    w1p = jnp.pad(w1, ((0, LANES - emb), (0, 0)))
    max_decay = math.log(DECAY_TARGET) / FAST_DECAY_PCT
    min_decay = math.log(DECAY_TARGET) / SLOW_DECAY_PCT
    deltas = jnp.abs(jnp.linspace(min_decay, max_decay, c, dtype=F32)).reshape(1, c)
    full = lambda a: pl.BlockSpec(a.shape, lambda i: (0,) * a.ndim)
    half_blocks = seq_len // tl
    pre = [w1p, b1.reshape(1, fw), w2, b2.reshape(1, fw), w3, b3.reshape(1, fw)]
    post = [freq.reshape(1, fw), deltas]
    kern = functools.partial(_filter_kernel, seq_len=seq_len)
    return pl.pallas_call(
        kern,
        grid=(2 * half_blocks,),
        in_specs=([pl.BlockSpec((tl, LANES), lambda i: (i, 0))] + [full(a) for a in pre]
                  + [pl.BlockSpec((fw, c), lambda i: (0, i // half_blocks))]
                  + [full(a) for a in post]),
        out_specs=pl.BlockSpec((tl, c), lambda i: (i, 0)),
        out_shape=jax.ShapeDtypeStruct((2 * seq_len, c), BF16),
        compiler_params=_cparams("parallel"),
        name="hyena_filters",
    )(z, *pre, w4, *post)


def _dft_outer_kernel(w_ref, x_ref, or_ref, oi_ref, xs_ref, os_ref):
    k, tm2, tc = x_ref.shape
    n1 = or_ref.shape[0]
    groups = tc // LANES
    x = x_ref[...].astype(F32).reshape(k * tm2, tc)
    for g in range(groups):
        xs_ref[g] = x[:, g * LANES:(g + 1) * LANES]
    w = w_ref[...]
    for j in range(tm2):
        xj = jnp.concatenate([xs_ref[g, pl.ds(j, k, stride=tm2), :] for g in range(groups)], axis=1)
        res = jnp.dot(w, xj.astype(BF16), preferred_element_type=F32)
        for g in range(groups):
            os_ref[g, pl.ds(j, 2 * n1, stride=tm2), :] = res[:, g * LANES:(g + 1) * LANES]
    for g in range(groups):
        o = os_ref[g].astype(or_ref.dtype).reshape(2 * n1, tm2, LANES)
        or_ref[:, :, g * LANES:(g + 1) * LANES] = o[:n1]
        oi_ref[:, :, g * LANES:(g + 1) * LANES] = o[n1:]


def _dft_outer(w, x4, tm2, tc):
    g, k, n2, c = x4.shape
    n1 = w.shape[0] // 2
    out = jax.ShapeDtypeStruct((g, n1, n2, c), BF16)
    return pl.pallas_call(
        _dft_outer_kernel,
        grid=(g, n2 // tm2, c // tc),
        in_specs=[pl.BlockSpec(w.shape, lambda a, i, j: (0, 0)),
                  pl.BlockSpec((None, k, tm2, tc), lambda a, i, j: (a, 0, i, j))],
        out_specs=[pl.BlockSpec((None, n1, tm2, tc), lambda a, i, j: (a, 0, i, j)),
                   pl.BlockSpec((None, n1, tm2, tc), lambda a, i, j: (a, 0, i, j))],
        out_shape=[out, out],
        scratch_shapes=[pltpu.VMEM((tc // LANES, k * tm2, LANES), F32),
                        pltpu.VMEM((tc // LANES, 2 * n1 * tm2, LANES), F32)],
        compiler_params=_cparams("parallel", "parallel", "parallel"),
        name="dft_outer",
    )(w, x4)


def _cmul_tiled(ar, ai, tr, ti, out_r, out_i):
    rows = ar.shape[0]
    (r_ref, r0), (i_ref, i0) = out_r, out_i
    for cb in range(ar.shape[1] // LANES):
        sl = slice(cb * LANES, (cb + 1) * LANES)
        xr = ar[:, sl]
        xi = ai[:, sl]
        r_ref[r0:r0 + rows, sl] = (xr * tr - xi * ti).astype(r_ref.dtype)
        i_ref[i0:i0 + rows, sl] = (xr * ti + xi * tr).astype(i_ref.dtype)


def _spectrum_kernel(ar_ref, ai_ref, tr_ref, ti_ref, wf_ref, o_ref, s_ref, *, scale):
    n2 = tr_ref.shape[0]
    _cmul_tiled(ar_ref[...].astype(F32), ai_ref[...].astype(F32), tr_ref[...], ti_ref[...],
                (s_ref, 0), (s_ref, n2))
    o_ref[...] = jnp.dot(wf_ref[...], s_ref[...], preferred_element_type=F32) * scale


def _filter_spectrum(ar, ai, tw_r, tw_i, wf, scale):
    n1, n2, c = ar.shape
    r = 2 * n2
    kern = functools.partial(_spectrum_kernel, scale=scale)
    slab = pl.BlockSpec((None, n2, c), lambda k: (k, 0, 0))
    return pl.pallas_call(
        kern,
        grid=(n1,),
        in_specs=[slab, slab,
                  pl.BlockSpec((None, n2, LANES), lambda k: (k, 0, 0)),
                  pl.BlockSpec((None, n2, LANES), lambda k: (k, 0, 0)),
                  pl.BlockSpec((r, r), lambda k: (0, 0))],
        out_specs=pl.BlockSpec((None, r, c), lambda k: (k, 0, 0)),
        out_shape=jax.ShapeDtypeStruct((n1, r, c), F32),
        scratch_shapes=[pltpu.VMEM((r, c), BF16)],
        compiler_params=_cparams("parallel"),
        name="filter_spectrum",
    )(ar, ai, tw_r, tw_i, wf)


def _fft_mid_kernel(ar_ref, ai_ref, h_ref, tr_ref, ti_ref, wf_ref, wi_ref, zr_ref, zi_ref, s_ref):
    n2 = tr_ref.shape[0]
    tr = tr_ref[...]
    ti = ti_ref[...]
    _cmul_tiled(ar_ref[...].astype(F32), ai_ref[...].astype(F32), tr, ti,
                (s_ref, 0), (s_ref, n2))
    x = jnp.dot(wf_ref[...], s_ref[...], preferred_element_type=F32)
    h = h_ref[...]
    xr, xi = x[:n2], x[n2:]
    hr, hi = h[:n2], h[n2:]
    s_ref[0:n2, :] = (xr * hr - xi * hi).astype(s_ref.dtype)
    s_ref[n2:2 * n2, :] = (xr * hi + xi * hr).astype(s_ref.dtype)
    z = jnp.dot(wi_ref[...], s_ref[...], preferred_element_type=F32)
    _cmul_tiled(z[:n2], z[n2:], tr, -ti, (zr_ref, 0), (zi_ref, 0))


def _fft_mid(ar, ai, hspec, tw_r, tw_i, wf, wi):
    batch, n1, n2, c = ar.shape
    r = 2 * n2
    slab = pl.BlockSpec((None, None, n2, c), lambda k, b: (b, k, 0, 0))
    out = jax.ShapeDtypeStruct((batch, n1, n2, c), BF16)
    return pl.pallas_call(
        _fft_mid_kernel,
        grid=(n1, batch),
        in_specs=[slab, slab,
                  pl.BlockSpec((None, r, c), lambda k, b: (k, 0, 0)),
                  pl.BlockSpec((None, n2, LANES), lambda k, b: (k, 0, 0)),
                  pl.BlockSpec((None, n2, LANES), lambda k, b: (k, 0, 0)),
                  pl.BlockSpec((r, r), lambda k, b: (0, 0)),
                  pl.BlockSpec((r, r), lambda k, b: (0, 0))],
        out_specs=[slab, slab],
        out_shape=[out, out],
        scratch_shapes=[pltpu.VMEM((r, c), BF16)],
        compiler_params=_cparams("parallel", "arbitrary"),
        name="fft_mid",
    )(ar, ai, hspec, tw_r, tw_i, wf, wi)


def _fft_out_kernel(w_ref, zr_ref, zi_ref, u_ref, x0_ref, b_ref, o_ref, rs_ref, is_ref, ys_ref):
    n1, tm2, tc = zr_ref.shape
    n1h = o_ref.shape[0]
    groups = tc // LANES
    zr = zr_ref[...].astype(F32).reshape(n1 * tm2, tc)
    zi = zi_ref[...].astype(F32).reshape(n1 * tm2, tc)
    for g in range(groups):
        rs_ref[g] = zr[:, g * LANES:(g + 1) * LANES]
        is_ref[g] = zi[:, g * LANES:(g + 1) * LANES]
    w = w_ref[...]
    for j in range(tm2):
        zj = jnp.concatenate(
            [jnp.concatenate([rs_ref[g, pl.ds(j, n1, stride=tm2), :] for g in range(groups)], axis=1),
             jnp.concatenate([is_ref[g, pl.ds(j, n1, stride=tm2), :] for g in range(groups)], axis=1)],
            axis=0)
        y = jnp.dot(w, zj.astype(BF16), preferred_element_type=F32)
        for g in range(groups):
            ys_ref[g, pl.ds(j, n1h, stride=tm2), :] = y[:, g * LANES:(g + 1) * LANES]
    for g in range(groups):
        sl = slice(g * LANES, (g + 1) * LANES)
        y = ys_ref[g].reshape(n1h, tm2, LANES)
        u = u_ref[:, :, sl].astype(F32)
        o_ref[:, :, sl] = ((y + u * b_ref[:, sl]) * x0_ref[:, :, sl].astype(F32)).astype(o_ref.dtype)


def _fft_out(w, zr, zi, u4, x04, bias, tm2, tc):
    batch, n1, n2, c = zr.shape
    n1h = w.shape[0]
    zspec = pl.BlockSpec((None, n1, tm2, tc), lambda a, i, j: (a, 0, i, j))
    uspec = pl.BlockSpec((None, n1h, tm2, tc), lambda a, i, j: (a, 0, i, j))
    return pl.pallas_call(
        _fft_out_kernel,
        grid=(batch, n2 // tm2, c // tc),
        in_specs=[pl.BlockSpec(w.shape, lambda a, i, j: (0, 0)), zspec, zspec, uspec, uspec,
                  pl.BlockSpec((1, tc), lambda a, i, j: (0, j))],
        out_specs=uspec,
        out_shape=jax.ShapeDtypeStruct((batch, n1h, n2, c), BF16),
        scratch_shapes=[pltpu.VMEM((tc // LANES, n1 * tm2, LANES), F32),
                        pltpu.VMEM((tc // LANES, n1 * tm2, LANES), F32),
                        pltpu.VMEM((tc // LANES, n1h * tm2, LANES), F32)],
        compiler_params=_cparams("parallel", "parallel", "parallel"),
        name="fft_out",
    )(w, zr, zi, u4, x04, bias.reshape(1, c))


def _dft_tables(seq_len):
    n = 2 * seq_len
    n2 = DFT_INNER
    n1 = n // n2
    n1h = n1 // 2
    k1 = np.arange(n1)[:, None]
    m1 = np.arange(n1)[None, :]
    ang1 = 2.0 * np.pi * k1 * m1 / n1
    w1_full = np.concatenate([np.cos(ang1), -np.sin(ang1)], axis=0)
    w1_half = w1_full[:, :n1h]
    ang1i = ang1.T[:n1h]
    w1_inv = np.concatenate([np.cos(ang1i), -np.sin(ang1i)], axis=1)
    k2 = np.arange(n2)[:, None]
    m2 = np.arange(n2)[None, :]
    ang2 = 2.0 * np.pi * k2 * m2 / n2
    c, s = np.cos(ang2), np.sin(ang2)
    w2_fwd = np.block([[c, s], [-s, c]])
    w2_inv = np.block([[c, -s], [s, c]])
    angt = 2.0 * np.pi * np.arange(n1)[:, None] * np.arange(n2)[None, :] / n
    tw_r = np.repeat(np.cos(angt)[:, :, None], LANES, axis=2)
    tw_i = np.repeat(-np.sin(angt)[:, :, None], LANES, axis=2)
    bf = lambda a: jnp.asarray(a, F32).astype(BF16)
    return dict(n1=n1, n2=n2, n1h=n1h, w1_full=bf(w1_full), w1_half=bf(w1_half), w1_inv=bf(w1_inv),
                w2_fwd=bf(w2_fwd), w2_inv=bf(w2_inv),
                tw_r=jnp.asarray(tw_r, F32), tw_i=jnp.asarray(tw_i, F32))


def _hyena_long_conv(u, x0, circ, bias, batch, seq_len, tab, tm2, tc):
    c = u.shape[1]
    n1, n2, n1h = tab["n1"], tab["n2"], tab["n1h"]
    tc = _pick(c, tc)
    far, fai = _dft_outer(tab["w1_full"], circ.reshape(1, n1, n2, c), tm2, tc)
    hspec = _filter_spectrum(far[0], fai[0], tab["tw_r"], tab["tw_i"], tab["w2_fwd"], 1.0 / (n1 * n2))
    u4 = u.reshape(batch, n1h, n2, c)
    x04 = x0.reshape(batch, n1h, n2, c)
    ar, ai = _dft_outer(tab["w1_half"], u4, tm2, tc)
    zr, zi = _fft_mid(ar, ai, hspec, tab["tw_r"], tab["tw_i"], tab["w2_fwd"], tab["w2_inv"])
    y = _fft_out(tab["w1_inv"], zr, zi, u4, x04, bias, tm2, tc)
    return y.reshape(batch * seq_len, c)


def _merge_kernel(attn_ref, hy_ref, gate_ref, x_ref, wa_ref, wh_ref, wo_ref, g2_ref,
                  x1_ref, hn_ref):
    d = x_ref.shape[1]
    pa = jnp.dot(attn_ref[...], wa_ref[...], preferred_element_type=F32)
    ph = jnp.dot(hy_ref[...], wh_ref[...], preferred_element_type=F32)
    merged = gate_ref[:, :d].astype(F32) * pa + gate_ref[:, d:].astype(F32) * ph
    x1 = x_ref[...] + jnp.dot(merged.astype(BF16), wo_ref[...], preferred_element_type=F32)
    x1_ref[...] = x1
    ms = jnp.mean(x1 * x1, axis=-1, keepdims=True)
    hn_ref[...] = (x1 * lax.rsqrt(ms + NORM_EPS) * g2_ref[...]).astype(hn_ref.dtype)


def _merge(attn, hyena, gates, x, wa, wh, wo, norm2, tm):
    t, d = x.shape
    const = lambda a: pl.BlockSpec(a.shape, lambda i: (0, 0), pipeline_mode=pl.Buffered(1))
    row = lambda n: pl.BlockSpec((tm, n), lambda i: (i, 0))
    g2 = norm2.reshape(1, d)
    return pl.pallas_call(
        _merge_kernel,
        grid=(t // tm,),
        in_specs=[row(attn.shape[1]), row(hyena.shape[1]), row(gates.shape[1]), row(d),
                  const(wa), const(wh), const(wo), const(g2)],
        out_specs=[row(d), row(d)],
        out_shape=[jax.ShapeDtypeStruct((t, d), F32), jax.ShapeDtypeStruct((t, d), BF16)],
        compiler_params=_cparams("parallel"),
        name="merge_out_proj",
    )(attn, hyena, gates, x, wa, wh, wo, g2)


def _ffn_kernel(prev_ref, main_ref, next_ref, x1_ref, wg_ref, wv_ref, cg_ref, cv_ref, bg_ref, bv_ref,
                wd_ref, gf_ref, o_ref, ext_ref, acc_ref, *, blocks_per_seq):
    i = pl.program_id(0)
    j = pl.program_id(1)
    halo = prev_ref.shape[0]
    tm = main_ref.shape[0]

    @pl.when(j == 0)
    def _():
        first = (i % blocks_per_seq) == 0
        last = (i % blocks_per_seq) == blocks_per_seq - 1
        zero = jnp.zeros(prev_ref.shape, prev_ref.dtype)
        ext_ref[0:halo, :] = jnp.where(first, zero, prev_ref[...])
        ext_ref[halo:halo + tm, :] = main_ref[...]
        ext_ref[halo + tm:, :] = jnp.where(last, zero, next_ref[...])
        acc_ref[...] = jnp.zeros(acc_ref.shape, F32)

    ext = ext_ref[...]
    ug = jnp.dot(ext, wg_ref[...], preferred_element_type=F32)
    uv = jnp.dot(ext, wv_ref[...], preferred_element_type=F32)
    ug = _conv3_rows(ug, cg_ref[...], tm, halo) + bg_ref[...]
    uv = _conv3_rows(uv, cv_ref[...], tm, halo) + bv_ref[...]
    act = (ug / (1.0 + jnp.exp(-ug))) * uv
    acc_ref[...] += jnp.dot(act.astype(BF16), wd_ref[...], preferred_element_type=F32)

    @pl.when(j == pl.num_programs(1) - 1)
    def _():
        y = x1_ref[...] + acc_ref[...]
        ms = jnp.mean(y * y, axis=-1, keepdims=True)
        o_ref[...] = y * lax.rsqrt(ms + NORM_EPS) * gf_ref[...]


def _ffn(hn2, x1, w_up, conv_w, conv_b, w_down, norm_f, seq_len, tm, tf):
    t, d = x1.shape
    dff = w_down.shape[0]
    nj = dff // tf
    halo = BF16_SUBLANES
    r = tm // halo
    nb = t // halo
    kern = functools.partial(_ffn_kernel, blocks_per_seq=seq_len // tm)
    cb = conv_b.reshape(1, 2 * dff)
    return pl.pallas_call(
        kern,
        grid=(t // tm, nj),
        in_specs=[pl.BlockSpec((halo, d), lambda i, j: (jnp.maximum(i * r - 1, 0), 0)),
                  pl.BlockSpec((tm, d), lambda i, j: (i, 0)),
                  pl.BlockSpec((halo, d), lambda i, j: (jnp.minimum((i + 1) * r, nb - 1), 0)),
                  pl.BlockSpec((tm, d), lambda i, j: (i, 0)),
                  pl.BlockSpec((d, tf), lambda i, j: (0, j)),
                  pl.BlockSpec((d, tf), lambda i, j: (0, nj + j)),
                  pl.BlockSpec((3, tf), lambda i, j: (0, j)),
                  pl.BlockSpec((3, tf), lambda i, j: (0, nj + j)),
                  pl.BlockSpec((1, tf), lambda i, j: (0, j)),
                  pl.BlockSpec((1, tf), lambda i, j: (0, nj + j)),
                  pl.BlockSpec((tf, d), lambda i, j: (j, 0)),
                  pl.BlockSpec((1, d), lambda i, j: (0, 0))],
        out_specs=pl.BlockSpec((tm, d), lambda i, j: (i, 0)),
        out_shape=jax.ShapeDtypeStruct((t, d), F32),
        scratch_shapes=[pltpu.VMEM((tm + 2 * halo, d), BF16), pltpu.VMEM((tm, d), F32)],
        compiler_params=_cparams("parallel", "arbitrary"),
        name="conv_glu_mlp",
    )(hn2, hn2, hn2, x1, w_up, w_up, conv_w, conv_w, cb, cb, w_down, norm_f.reshape(1, d))


def _tiles(seq_len, d_model, d_ff):
    return dict(
        tm_norm=_pick(seq_len, 512),
        tm_proj=_pick(seq_len, 1024),
        tn_proj=1024,
        tq=_pick(seq_len, 512),
        tk=_pick(seq_len, 512),
        tm_prep=_pick(seq_len, 512),
        tl_filt=_pick(seq_len, 512),
        tm2_dft=BF16_SUBLANES,
        tc_dft=256,
        tm_merge=_pick(seq_len, 256),
        tm_ffn=_pick(seq_len, 512),
        tf_ffn=_pick(d_ff, 512),
    )


def _encoder_layer(x, batch, seq_len, lambda_init, p, consts):
    t, d = x.shape
    tl = _tiles(seq_len, d, p["w_down"].shape[0])
    hn = _rmsnorm(x, p["norm1"], tl["tm_norm"])

    qk = _proj_rope(hn, p["w_qk"], consts["cos"], consts["sin"], seq_len, tl["tm_proj"])
    v = _proj_plain(hn, p["w_v"], tl["tm_proj"], _pick(p["w_v"].shape[1], tl["tn_proj"]))
    hy = _proj_plain(hn, p["w_hy"], tl["tm_proj"], _pick(p["w_hy"].shape[1], tl["tn_proj"]))
    gates = _proj_gate(hn, p["w_gl"], p["gate_b"], tl["tm_proj"], _pick(p["w_gl"].shape[1], tl["tn_proj"]))

    attn = _attention(qk.reshape(batch, seq_len, -1), v.reshape(batch, seq_len, -1),
                      p["lambda_q1"], p["lambda_k1"], p["lambda_q2"], p["lambda_k2"], p["subln_g"],
                      batch, seq_len, lambda_init, tl["tq"], tl["tk"]).reshape(t, ATTN_WIDTH)

    u, x0 = _hyena_prep(hy, p["in_conv_w"], p["in_conv_b"], seq_len, tl["tm_prep"])
    circ = _hyena_filters(seq_len, p["filt_w1"], p["filt_b1"], p["filt_w2"], p["filt_b2"],
                          p["filt_w3"], p["filt_b3"], p["filt_w4"], p["filt_freq"], tl["tl_filt"])
    hyena = _hyena_long_conv(u, x0, circ, p["hyena_bias"], batch, seq_len, consts["dft"],
                             tl["tm2_dft"], tl["tc_dft"])

    x1, hn2 = _merge(attn, hyena, gates, x, p["w_attn_out"], p["w_hyena_out"], p["w_out"], p["norm2"],
                     tl["tm_merge"])
    return x1, hn2, tl


def kernel(x_prompt, x_sample, norm1, w_in, in_conv_w, in_conv_b, gate_b, lambda_q1, lambda_k1, lambda_q2, lambda_k2, subln_g, filt_w1, filt_b1, filt_w2, filt_b2, filt_w3, filt_b3, filt_w4, filt_freq, hyena_bias, w_attn_out, w_hyena_out, w_out, norm2, w_up, ffn_conv_w, ffn_conv_b, w_down, norm_f):
    depth = norm1.shape[0]
    assert depth == 1, "this kernel implements the single-layer trunk"
    d_model = x_prompt.shape[-1]
    hyena_width = hyena_bias.shape[-1]
    o1 = ATTN_QK_WIDTH
    o2 = o1 + ATTN_QK_WIDTH
    o3 = o2 + ATTN_WIDTH
    o4 = o3 + 3 * hyena_width

    layers = []
    for i in range(depth):
        wi = w_in[i].astype(BF16)
        layers.append(dict(
            norm1=norm1[i], w_qk=wi[:, :o2], w_v=wi[:, o2:o3], w_hy=wi[:, o3:o4], w_gl=wi[:, o4:],
            in_conv_w=in_conv_w[i], in_conv_b=in_conv_b[i], gate_b=gate_b[i],
            lambda_q1=lambda_q1[i], lambda_k1=lambda_k1[i], lambda_q2=lambda_q2[i], lambda_k2=lambda_k2[i],
            subln_g=subln_g[i],
            filt_w1=filt_w1[i], filt_b1=filt_b1[i], filt_w2=filt_w2[i], filt_b2=filt_b2[i],
            filt_w3=filt_w3[i], filt_b3=filt_b3[i], filt_w4=filt_w4[i], filt_freq=filt_freq[i],
            hyena_bias=hyena_bias[i],
            w_attn_out=w_attn_out[i].astype(BF16), w_hyena_out=w_hyena_out[i].astype(BF16),
            w_out=w_out[i].astype(BF16), norm2=norm2[i],
            w_up=w_up[i].astype(BF16), ffn_conv_w=ffn_conv_w[i], ffn_conv_b=ffn_conv_b[i],
            w_down=w_down[i].astype(BF16)))

    def trunk(x3):
        batch, seq_len, _ = x3.shape
        cos, sin = _rope_tables(seq_len)
        consts = dict(cos=cos, sin=sin, dft=_dft_tables(seq_len))
        x = x3.reshape(batch * seq_len, d_model)
        lambda_init = 0.8 - 0.6 * math.exp(-0.3 * 0)
        p = layers[0]
        x1, hn2, tl = _encoder_layer(x, batch, seq_len, lambda_init, p, consts)
        y = _ffn(hn2, x1, p["w_up"], p["ffn_conv_w"], p["ffn_conv_b"], p["w_down"], norm_f, seq_len,
                 tl["tm_ffn"], tl["tf_ffn"])
        return y.reshape(batch, seq_len, d_model)

    return (trunk(x_prompt), trunk(x_sample))
```

```python
import functools
import math

import numpy as np
import jax
import jax.numpy as jnp
from jax import lax
from jax.experimental import pallas as pl
from jax.experimental.pallas import tpu as pltpu

ATTN_HEADS = 8
ATTN_QK_DIM = 64
ATTN_V_DIM = 2 * ATTN_QK_DIM
ATTN_QK_WIDTH = ATTN_HEADS * 2 * ATTN_QK_DIM
ATTN_WIDTH = ATTN_HEADS * ATTN_V_DIM
HYENA_EMB_DIM = 33
HYENA_BANDS = (HYENA_EMB_DIM - 1) // 2
DECAY_TARGET = 1e-2
FAST_DECAY_PCT = 0.3
SLOW_DECAY_PCT = 1.5
ROPE_THETA = 10000.0
NORM_EPS = 1e-6

LANES = 128
BF16_SUBLANES = 16
VMEM_LIMIT_BYTES = 56 * 1024 * 1024

DFT_INNER = LANES

F32 = jnp.float32
BF16 = jnp.bfloat16


def _cparams(*sem):
    return pltpu.CompilerParams(dimension_semantics=sem, vmem_limit_bytes=VMEM_LIMIT_BYTES)


def _pick(n, pref):
    t = min(n, pref)
    while n % t:
        t //= 2
    return t


def _rmsnorm_kernel(x_ref, g_ref, o_ref):
    x = x_ref[...]
    ms = jnp.mean(x * x, axis=-1, keepdims=True)
    o_ref[...] = (x * lax.rsqrt(ms + NORM_EPS) * g_ref[...]).astype(o_ref.dtype)


def _rmsnorm(x, g, tm):
    t, d = x.shape
    return pl.pallas_call(
        _rmsnorm_kernel,
        grid=(t // tm,),
        in_specs=[pl.BlockSpec((tm, d), lambda i: (i, 0)),
                  pl.BlockSpec((1, d), lambda i: (0, 0))],
        out_specs=pl.BlockSpec((tm, d), lambda i: (i, 0)),
        out_shape=jax.ShapeDtypeStruct((t, d), BF16),
        compiler_params=_cparams("parallel"),
        name="rmsnorm_in",
    )(x, g.reshape(1, d))


def _proj_plain_kernel(h_ref, w_ref, o_ref):
    o_ref[...] = jnp.dot(h_ref[...], w_ref[...], preferred_element_type=F32).astype(o_ref.dtype)


def _proj_plain(h, w, tm, tn):
    t, d = h.shape
    n = w.shape[1]
    return pl.pallas_call(
        _proj_plain_kernel,
        grid=(t // tm, n // tn),
        in_specs=[pl.BlockSpec((tm, d), lambda i, j: (i, 0)),
                  pl.BlockSpec((d, tn), lambda i, j: (0, j))],
        out_specs=pl.BlockSpec((tm, tn), lambda i, j: (i, j)),
        out_shape=jax.ShapeDtypeStruct((t, n), BF16),
        compiler_params=_cparams("parallel", "arbitrary"),
        name="proj_plain",
    )(h, w)


def _proj_gate_kernel(h_ref, w_ref, b_ref, o_ref):
    z = jnp.dot(h_ref[...], w_ref[...], preferred_element_type=F32) + b_ref[...]
    o_ref[...] = (1.0 / (1.0 + jnp.exp(-z))).astype(o_ref.dtype)


def _proj_gate(h, w, b, tm, tn):
    t, d = h.shape
    n = w.shape[1]
    return pl.pallas_call(
        _proj_gate_kernel,
        grid=(t // tm, n // tn),
        in_specs=[pl.BlockSpec((tm, d), lambda i, j: (i, 0)),
                  pl.BlockSpec((d, tn), lambda i, j: (0, j)),
                  pl.BlockSpec((1, tn), lambda i, j: (0, j))],
        out_specs=pl.BlockSpec((tm, tn), lambda i, j: (i, j)),
        out_shape=jax.ShapeDtypeStruct((t, n), BF16),
        compiler_params=_cparams("parallel", "arbitrary"),
        name="proj_gate",
    )(h, w, b.reshape(1, n))


def _proj_rope_kernel(h_ref, w_ref, cos_ref, sin_ref, o_ref, *, q_scale):
    acc = jnp.dot(h_ref[...], w_ref[...], preferred_element_type=F32)
    cos = cos_ref[...]
    sin = sin_ref[...]
    lane = lax.broadcasted_iota(jnp.int32, cos.shape, 1)
    first_half = (lane % ATTN_QK_DIM) < (ATTN_QK_DIM // 2)
    scale = jnp.where(pl.program_id(1) == 0, q_scale, 1.0).astype(F32)
    half = ATTN_QK_DIM // 2
    for g in range(acc.shape[1] // LANES):
        x = acc[:, g * LANES:(g + 1) * LANES]
        up = pltpu.roll(x, LANES - half, 1)
        down = pltpu.roll(x, half, 1)
        rot = jnp.where(first_half, up, down)
        o_ref[:, g * LANES:(g + 1) * LANES] = ((x * cos + rot * sin) * scale).astype(o_ref.dtype)


def _proj_rope(h, w, cos, sin, seq_len, tm):
    t, d = h.shape
    n = w.shape[1]
    tn = ATTN_QK_WIDTH
    nblk = seq_len // tm
    kern = functools.partial(_proj_rope_kernel, q_scale=ATTN_QK_DIM ** -0.5 * math.log2(math.e))
    return pl.pallas_call(
        kern,
        grid=(t // tm, n // tn),
        in_specs=[pl.BlockSpec((tm, d), lambda i, j: (i, 0)),
                  pl.BlockSpec((d, tn), lambda i, j: (0, j)),
                  pl.BlockSpec((tm, LANES), lambda i, j: (i % nblk, 0)),
                  pl.BlockSpec((tm, LANES), lambda i, j: (i % nblk, 0))],
        out_specs=pl.BlockSpec((tm, tn), lambda i, j: (i, j)),
        out_shape=jax.ShapeDtypeStruct((t, n), BF16),
        compiler_params=_cparams("parallel", "arbitrary"),
        name="proj_rope",
    )(h, w, cos, sin)


def _rope_tables(seq_len):
    d = ATTN_QK_DIM
    inv = ROPE_THETA ** (-jnp.arange(0, d, 2, dtype=F32) / d)
    ang = jnp.arange(seq_len, dtype=F32)[:, None] * inv[None]
    ang = jnp.concatenate([ang, ang], -1)
    sign = jnp.concatenate([-jnp.ones((d // 2,), F32), jnp.ones((d // 2,), F32)])
    cos = jnp.cos(ang)
    sin = jnp.sin(ang) * sign
    reps = LANES // d
    return jnp.tile(cos, (1, reps)), jnp.tile(sin, (1, reps))


def _attn_kernel(q_ref, k_ref, v_ref, lq1_ref, lk1_ref, lq2_ref, lk2_ref, g_ref, o_ref,
                 sa0_ref, sb0_ref, sa1_ref, sb1_ref, m0_ref, m1_ref, acc0_ref, acc1_ref,
                 *, tq, tk, lambda_init):
    seq = k_ref.shape[0]
    n_chunks = seq // tk
    n_tiles = seq // tq
    lane = lax.broadcasted_iota(jnp.int32, (tq, LANES), 1)
    ones_col = (lax.broadcasted_iota(jnp.int32, (tk, LANES), 1) == 0).astype(BF16)
    nt = (((1,), (1,)), ((), ()))
    score_refs = ((sa0_ref, sb0_ref), (sa1_ref, sb1_ref))
    lam = (jnp.exp(jnp.sum(lq1_ref[...] * lk1_ref[...], axis=1, keepdims=True))
           - jnp.exp(jnp.sum(lq2_ref[...] * lk2_ref[...], axis=1, keepdims=True))
           + lambda_init)

    def tile_rows(tile):
        start = tile * tq if isinstance(tile, int) else pl.multiple_of(tile * tq, tq)
        return pl.ds(start, tq)

    def split_maps(tile):
        q = q_ref[tile_rows(tile), :]
        zero = jnp.zeros_like(q)
        return jnp.where(lane < ATTN_QK_DIM, q, zero), jnp.where(lane >= ATTN_QK_DIM, q, zero)

    def scores(qmaps, j, slot):
        kc = k_ref[j * tk:(j + 1) * tk, :]
        sa_ref, sb_ref = score_refs[slot]
        sa_ref[...] = lax.dot_general(qmaps[0], kc, nt, preferred_element_type=F32)
        sb_ref[...] = lax.dot_general(qmaps[1], kc, nt, preferred_element_type=F32)

    def one_map(s_ref, vext, m_ref, acc_ref, first):
        s = s_ref[...]
        m_cur = jnp.max(s, axis=1, keepdims=True)
        if first:
            m_new = jnp.broadcast_to(m_cur, m_ref.shape)
        else:
            m_old = m_ref[...]
            m_new = jnp.maximum(m_old, m_cur)
        p = jnp.concatenate(
            [jnp.exp2(s[:, c * LANES:(c + 1) * LANES] - m_new) for c in range(tk // LANES)],
            axis=1).astype(BF16)
        pv = jnp.dot(p, vext, preferred_element_type=F32)
        if first:
            acc_ref[...] = pv
        else:
            alpha = jnp.exp2(m_old - m_new)
            acc_ref[...] = jnp.concatenate([alpha, alpha], axis=1) * acc_ref[...] + pv
        m_ref[...] = m_new

    def weighted_values(j, slot):
        vext = jnp.concatenate([v_ref[j * tk:(j + 1) * tk, :], ones_col], axis=1)
        sa_ref, sb_ref = score_refs[slot]
        one_map(sa_ref, vext, m0_ref, acc0_ref, j == 0)
        one_map(sb_ref, vext, m1_ref, acc1_ref, j == 0)

    def finish(tile):
        a0 = acc0_ref[...]
        a1 = acc1_ref[...]
        o0 = a0[:, :LANES] / a0[:, LANES:LANES + 1]
        o1 = a1[:, :LANES] / a1[:, LANES:LANES + 1]
        o = o0 - lam * o1
        ms = jnp.mean(o * o, axis=-1, keepdims=True)
        o = o * lax.rsqrt(ms + NORM_EPS) * g_ref[...]
        o_ref[tile_rows(tile), :] = (o * (1.0 - lambda_init)).astype(o_ref.dtype)

    scores(split_maps(0), 0, 0)
    acc0_ref[...] = jnp.ones(acc0_ref.shape, F32)
    acc1_ref[...] = jnp.ones(acc1_ref.shape, F32)

    def tile_body(tile, carry):
        qmaps = split_maps(tile)
        for j in range(0, n_chunks, 2):
            scores(qmaps, j + 1, 1)
            if j == 0:
                finish(jnp.maximum(tile - 1, 0))
            weighted_values(j, 0)
            if j + 2 < n_chunks:
                scores(qmaps, j + 2, 0)
            else:
                scores(split_maps(jnp.minimum(tile + 1, n_tiles - 1)), 0, 0)
            weighted_values(j + 1, 1)
        return carry

    lax.fori_loop(0, n_tiles, tile_body, 0)
    finish(n_tiles - 1)


def _attention(qk, v, lq1, lk1, lq2, lk2, subln_g, batch, seq_len, lambda_init, tq, tk):
    assert (seq_len // tk) % 2 == 0, "key chunks are processed in pairs"
    kern = functools.partial(_attn_kernel, tq=tq, tk=tk, lambda_init=lambda_init)
    small = pl.BlockSpec((1, ATTN_QK_DIM), lambda b, h: (0, 0))
    return pl.pallas_call(
        kern,
        grid=(batch, ATTN_HEADS),
        in_specs=[pl.BlockSpec((None, seq_len, LANES), lambda b, h: (b, 0, h)),
                  pl.BlockSpec((None, seq_len, LANES), lambda b, h: (b, 0, ATTN_HEADS + h)),
                  pl.BlockSpec((None, seq_len, LANES), lambda b, h: (b, 0, h)),
                  small, small, small, small,
                  pl.BlockSpec((1, LANES), lambda b, h: (0, 0))],
        out_specs=pl.BlockSpec((None, seq_len, LANES), lambda b, h: (b, 0, h)),
        out_shape=jax.ShapeDtypeStruct((batch, seq_len, ATTN_WIDTH), BF16),
        scratch_shapes=[pltpu.VMEM((tq, tk), F32)] * 4
                       + [pltpu.VMEM((tq, LANES), F32), pltpu.VMEM((tq, LANES), F32),
                          pltpu.VMEM((tq, 2 * LANES), F32), pltpu.VMEM((tq, 2 * LANES), F32)],
        compiler_params=_cparams("parallel", "parallel"),
        name="diff_attention",
    )(qk, qk, v, lq1.reshape(1, -1), lk1.reshape(1, -1), lq2.reshape(1, -1), lk2.reshape(1, -1),
      subln_g.reshape(1, -1))


def _conv3_rows(e, w, n_rows, halo):
    tot = e.shape[0]
    prev = pltpu.roll(e, 1, 0)
    nxt = pltpu.roll(e, tot - 1, 0)
    c = prev * w[0:1, :] + e * w[1:2, :] + nxt * w[2:3, :]
    return c[halo:halo + n_rows, :]


def _hyena_prep_kernel(prev_ref, main_ref, next_ref, w_ref, b_ref, u_ref, x0_ref, *, blocks_per_seq):
    i = pl.program_id(0)
    first = (i % blocks_per_seq) == 0
    last = (i % blocks_per_seq) == blocks_per_seq - 1
    halo = prev_ref.shape[0]
    tm = main_ref.shape[0]
    prev = jnp.where(first, 0.0, prev_ref[...].astype(F32))
    nxt = jnp.where(last, 0.0, next_ref[...].astype(F32))
    e = jnp.concatenate([prev, main_ref[...].astype(F32), nxt], axis=0)
    c = _conv3_rows(e, w_ref[...], tm, halo) + b_ref[...]
    width = u_ref.shape[1]
    x0 = c[:, :width]
    x1 = c[:, width:2 * width]
    hv = c[:, 2 * width:]
    u_ref[...] = (hv * x1).astype(u_ref.dtype)
    x0_ref[...] = x0.astype(x0_ref.dtype)


def _hyena_prep(hy, conv_w, conv_b, seq_len, tm):
    t, n = hy.shape
    width = n // 3
    halo = BF16_SUBLANES
    r = tm // halo
    nb = t // halo
    kern = functools.partial(_hyena_prep_kernel, blocks_per_seq=seq_len // tm)
    return pl.pallas_call(
        kern,
        grid=(t // tm,),
        in_specs=[pl.BlockSpec((halo, n), lambda i: (jnp.maximum(i * r - 1, 0), 0)),
                  pl.BlockSpec((tm, n), lambda i: (i, 0)),
                  pl.BlockSpec((halo, n), lambda i: (jnp.minimum((i + 1) * r, nb - 1), 0)),
                  pl.BlockSpec((3, n), lambda i: (0, 0)),
                  pl.BlockSpec((1, n), lambda i: (0, 0))],
        out_specs=[pl.BlockSpec((tm, width), lambda i: (i, 0)),
                   pl.BlockSpec((tm, width), lambda i: (i, 0))],
        out_shape=[jax.ShapeDtypeStruct((t, width), BF16),
                   jax.ShapeDtypeStruct((t, width), BF16)],
        compiler_params=_cparams("parallel"),
        name="hyena_prep",
    )(hy, hy, hy, conv_w, conv_b.reshape(1, n))


def _filter_kernel(z_ref, w1_ref, b1_ref, w2_ref, b2_ref, w3_ref, b3_ref, w4_ref, f_ref, d_ref, o_ref,
                   *, seq_len):
    hp = lax.Precision.HIGHEST
    z = z_ref[...]
    f = f_ref[...]
    h = jnp.sin(f * (jnp.dot(z, w1_ref[...], precision=hp, preferred_element_type=F32) + b1_ref[...]))
    h = jnp.sin(f * (jnp.dot(h, w2_ref[...], precision=hp, preferred_element_type=F32) + b2_ref[...]))
    h = jnp.sin(f * (jnp.dot(h, w3_ref[...], precision=hp, preferred_element_type=F32) + b3_ref[...]))
    h = jnp.dot(h, w4_ref[...], precision=hp, preferred_element_type=F32)
    t = z[:, 0:1]
    h = h * jnp.exp(-t * d_ref[...])
    row = pl.program_id(0) * z.shape[0] + lax.broadcasted_iota(jnp.int32, (z.shape[0], 1), 0)
    o_ref[...] = jnp.where(row == seq_len, 0.0, h).astype(o_ref.dtype)


def _hyena_filters(seq_len, w1, b1, w2, b2, w3, b3, w4, freq, tl):
    fw = w1.shape[1]
    c = w4.shape[1] // 2
    t = jnp.linspace(0.0, 1.0, seq_len, dtype=F32)[:, None]
    w = 2.0 * math.pi * jnp.arange(seq_len, dtype=F32)[:, None] / seq_len
    f = jnp.linspace(1e-4, HYENA_BANDS - 1, HYENA_BANDS, dtype=F32)[None]
    z = jnp.concatenate([t, jnp.cos(f * w), -jnp.sin(f * w)], -1)
    emb = z.shape[1]
    z = jnp.concatenate([z, jnp.zeros((1, emb), F32), z[1:][::-1]], axis=0)
    z = jnp.pad(z, ((0, 0), (0, LANES - emb)))
    w1p = jnp.pad(w1, ((0, LANES - emb), (0, 0)))
    max_decay = math.log(DECAY_TARGET) / FAST_DECAY_PCT
    min_decay = math.log(DECAY_TARGET) / SLOW_DECAY_PCT
    deltas = jnp.abs(jnp.linspace(min_decay, max_decay, c, dtype=F32)).reshape(1, c)
    full = lambda a: pl.BlockSpec(a.shape, lambda i: (0,) * a.ndim)
    half_blocks = seq_len // tl
    pre = [w1p, b1.reshape(1, fw), w2, b2.reshape(1, fw), w3, b3.reshape(1, fw)]
    post = [freq.reshape(1, fw), deltas]
    kern = functools.partial(_filter_kernel, seq_len=seq_len)
    return pl.pallas_call(
        kern,
        grid=(2 * half_blocks,),
        in_specs=([pl.BlockSpec((tl, LANES), lambda i: (i, 0))] + [full(a) for a in pre]
                  + [pl.BlockSpec((fw, c), lambda i: (0, i // half_blocks))]
                  + [full(a) for a in post]),
        out_specs=pl.BlockSpec((tl, c), lambda i: (i, 0)),
        out_shape=jax.ShapeDtypeStruct((2 * seq_len, c), BF16),
        compiler_params=_cparams("parallel"),
        name="hyena_filters",
    )(z, *pre, w4, *post)


def _dft_outer_kernel(w_ref, x_ref, or_ref, oi_ref, xs_ref, os_ref):
    k, tm2, tc = x_ref.shape
    n1 = or_ref.shape[0]
    groups = tc // LANES
    x = x_ref[...].astype(F32).reshape(k * tm2, tc)
    for g in range(groups):
        xs_ref[g] = x[:, g * LANES:(g + 1) * LANES]
    w = w_ref[...]
    for j in range(tm2):
        xj = jnp.concatenate([xs_ref[g, pl.ds(j, k, stride=tm2), :] for g in range(groups)], axis=1)
        res = jnp.dot(w, xj.astype(BF16), preferred_element_type=F32)
        for g in range(groups):
            os_ref[g, pl.ds(j, 2 * n1, stride=tm2), :] = res[:, g * LANES:(g + 1) * LANES]
    for g in range(groups):
        o = os_ref[g].astype(or_ref.dtype).reshape(2 * n1, tm2, LANES)
        or_ref[:, :, g * LANES:(g + 1) * LANES] = o[:n1]
        oi_ref[:, :, g * LANES:(g + 1) * LANES] = o[n1:]


def _dft_outer(w, x4, tm2, tc):
    g, k, n2, c = x4.shape
    n1 = w.shape[0] // 2
    out = jax.ShapeDtypeStruct((g, n1, n2, c), BF16)
    return pl.pallas_call(
        _dft_outer_kernel,
        grid=(g, n2 // tm2, c // tc),
        in_specs=[pl.BlockSpec(w.shape, lambda a, i, j: (0, 0)),
                  pl.BlockSpec((None, k, tm2, tc), lambda a, i, j: (a, 0, i, j))],
        out_specs=[pl.BlockSpec((None, n1, tm2, tc), lambda a, i, j: (a, 0, i, j)),
                   pl.BlockSpec((None, n1, tm2, tc), lambda a, i, j: (a, 0, i, j))],
        out_shape=[out, out],
        scratch_shapes=[pltpu.VMEM((tc // LANES, k * tm2, LANES), F32),
                        pltpu.VMEM((tc // LANES, 2 * n1 * tm2, LANES), F32)],
        compiler_params=_cparams("parallel", "parallel", "parallel"),
        name="dft_outer",
    )(w, x4)


def _spectrum_kernel(ar_ref, ai_ref, wf_ref, o_ref, s_ref, *, scale):
    n2 = ar_ref.shape[0]
    s_ref[0:n2, :] = ar_ref[...]
    s_ref[n2:2 * n2, :] = ai_ref[...]
    o_ref[...] = jnp.dot(wf_ref[...], s_ref[...], preferred_element_type=F32) * scale


def _filter_spectrum(ar, ai, wf, scale):
    n1, n2, c = ar.shape
    r = 2 * n2
    kern = functools.partial(_spectrum_kernel, scale=scale)
    slab = pl.BlockSpec((None, n2, c), lambda k: (k, 0, 0))
    return pl.pallas_call(
        kern,
        grid=(n1,),
        in_specs=[slab, slab, pl.BlockSpec((None, r, r), lambda k: (k, 0, 0))],
        out_specs=pl.BlockSpec((None, r, c), lambda k: (k, 0, 0)),
        out_shape=jax.ShapeDtypeStruct((n1, r, c), F32),
        scratch_shapes=[pltpu.VMEM((r, c), BF16)],
        compiler_params=_cparams("parallel"),
        name="filter_spectrum",
    )(ar, ai, wf)


def _fft_mid_kernel(ar_ref, ai_ref, h_ref, wf_ref, wi_ref, zr_ref, zi_ref, s_ref):
    n2 = ar_ref.shape[0]
    s_ref[0:n2, :] = ar_ref[...]
    s_ref[n2:2 * n2, :] = ai_ref[...]
    x = jnp.dot(wf_ref[...], s_ref[...], preferred_element_type=F32)
    h = h_ref[...]
    xr, xi = x[:n2], x[n2:]
    hr, hi = h[:n2], h[n2:]
    s_ref[0:n2, :] = (xr * hr - xi * hi).astype(s_ref.dtype)
    s_ref[n2:2 * n2, :] = (xr * hi + xi * hr).astype(s_ref.dtype)
    z = jnp.dot(wi_ref[...], s_ref[...], preferred_element_type=F32)
    zr_ref[...] = z[:n2].astype(zr_ref.dtype)
    zi_ref[...] = z[n2:].astype(zi_ref.dtype)


def _fft_mid(ar, ai, hspec, wf, wi):
    batch, n1, n2, c = ar.shape
    r = 2 * n2
    slab = pl.BlockSpec((None, None, n2, c), lambda k, b: (b, k, 0, 0))
    out = jax.ShapeDtypeStruct((batch, n1, n2, c), BF16)
    return pl.pallas_call(
        _fft_mid_kernel,
        grid=(n1, batch),
        in_specs=[slab, slab,
                  pl.BlockSpec((None, r, c), lambda k, b: (k, 0, 0)),
                  pl.BlockSpec((None, r, r), lambda k, b: (k, 0, 0)),
                  pl.BlockSpec((None, r, r), lambda k, b: (k, 0, 0))],
        out_specs=[slab, slab],
        out_shape=[out, out],
        scratch_shapes=[pltpu.VMEM((r, c), BF16)],
        compiler_params=_cparams("parallel", "arbitrary"),
        name="fft_mid",
    )(ar, ai, hspec, wf, wi)


def _fft_out_kernel(w_ref, zr_ref, zi_ref, u_ref, x0_ref, b_ref, o_ref, rs_ref, is_ref, ys_ref):
    n1, tm2, tc = zr_ref.shape
    n1h = o_ref.shape[0]
    groups = tc // LANES
    zr = zr_ref[...].astype(F32).reshape(n1 * tm2, tc)
    zi = zi_ref[...].astype(F32).reshape(n1 * tm2, tc)
    for g in range(groups):
        rs_ref[g] = zr[:, g * LANES:(g + 1) * LANES]
        is_ref[g] = zi[:, g * LANES:(g + 1) * LANES]
    w = w_ref[...]
    for j in range(tm2):
        zj = jnp.concatenate(
            [jnp.concatenate([rs_ref[g, pl.ds(j, n1, stride=tm2), :] for g in range(groups)], axis=1),
             jnp.concatenate([is_ref[g, pl.ds(j, n1, stride=tm2), :] for g in range(groups)], axis=1)],
            axis=0)
        y = jnp.dot(w, zj.astype(BF16), preferred_element_type=F32)
        for g in range(groups):
            ys_ref[g, pl.ds(j, n1h, stride=tm2), :] = y[:, g * LANES:(g + 1) * LANES]
    for g in range(groups):
        sl = slice(g * LANES, (g + 1) * LANES)
        y = ys_ref[g].reshape(n1h, tm2, LANES)
        u = u_ref[:, :, sl].astype(F32)
        o_ref[:, :, sl] = ((y + u * b_ref[:, sl]) * x0_ref[:, :, sl].astype(F32)).astype(o_ref.dtype)


def _fft_out(w, zr, zi, u4, x04, bias, tm2, tc):
    batch, n1, n2, c = zr.shape
    n1h = w.shape[0]
    zspec = pl.BlockSpec((None, n1, tm2, tc), lambda a, i, j: (a, 0, i, j))
    uspec = pl.BlockSpec((None, n1h, tm2, tc), lambda a, i, j: (a, 0, i, j))
    return pl.pallas_call(
        _fft_out_kernel,
        grid=(batch, n2 // tm2, c // tc),
        in_specs=[pl.BlockSpec(w.shape, lambda a, i, j: (0, 0)), zspec, zspec, uspec, uspec,
                  pl.BlockSpec((1, tc), lambda a, i, j: (0, j))],
        out_specs=uspec,
        out_shape=jax.ShapeDtypeStruct((batch, n1h, n2, c), BF16),
        scratch_shapes=[pltpu.VMEM((tc // LANES, n1 * tm2, LANES), F32),
                        pltpu.VMEM((tc // LANES, n1 * tm2, LANES), F32),
                        pltpu.VMEM((tc // LANES, n1h * tm2, LANES), F32)],
        compiler_params=_cparams("parallel", "parallel", "parallel"),
        name="fft_out",
    )(w, zr, zi, u4, x04, bias.reshape(1, c))


def _dft_tables(seq_len):
    n = 2 * seq_len
    n2 = DFT_INNER
    n1 = n // n2
    n1h = n1 // 2
    k1 = np.arange(n1)[:, None]
    m1 = np.arange(n1)[None, :]
    ang1 = 2.0 * np.pi * k1 * m1 / n1
    w1_full = np.concatenate([np.cos(ang1), -np.sin(ang1)], axis=0)
    w1_half = w1_full[:, :n1h]
    ang1i = ang1.T[:n1h]
    w1_inv = np.concatenate([np.cos(ang1i), -np.sin(ang1i)], axis=1)
    k1d = jnp.arange(n1, dtype=jnp.int32)[:, None, None]
    k2d = jnp.arange(n2, dtype=jnp.int32)[None, :, None]
    m2d = jnp.arange(n2, dtype=jnp.int32)[None, None, :]
    phase = (m2d * (k1d + n1 * k2d)) % n
    ang = phase.astype(F32) * (2.0 * math.pi / n)
    c, s = jnp.cos(ang), jnp.sin(ang)
    w2_fwd = jnp.concatenate([jnp.concatenate([c, s], axis=2), jnp.concatenate([-s, c], axis=2)], axis=1)
    ct, st = jnp.swapaxes(c, 1, 2), jnp.swapaxes(s, 1, 2)
    w2_inv = jnp.concatenate([jnp.concatenate([ct, -st], axis=2), jnp.concatenate([st, ct], axis=2)], axis=1)
    bf = lambda a: jnp.asarray(a, F32).astype(BF16)
    return dict(n1=n1, n2=n2, n1h=n1h, w1_full=bf(w1_full), w1_half=bf(w1_half), w1_inv=bf(w1_inv),
                w2_fwd=w2_fwd.astype(BF16), w2_inv=w2_inv.astype(BF16))


def _hyena_long_conv(u, x0, circ, bias, batch, seq_len, tab, tm2, tc):
    c = u.shape[1]
    n1, n2, n1h = tab["n1"], tab["n2"], tab["n1h"]
    tc = _pick(c, tc)
    far, fai = _dft_outer(tab["w1_full"], circ.reshape(1, n1, n2, c), tm2, tc)
    hspec = _filter_spectrum(far[0], fai[0], tab["w2_fwd"], 1.0 / (n1 * n2))
    u4 = u.reshape(batch, n1h, n2, c)
    x04 = x0.reshape(batch, n1h, n2, c)
    ar, ai = _dft_outer(tab["w1_half"], u4, tm2, tc)
    zr, zi = _fft_mid(ar, ai, hspec, tab["w2_fwd"], tab["w2_inv"])
    y = _fft_out(tab["w1_inv"], zr, zi, u4, x04, bias, tm2, tc)
    return y.reshape(batch * seq_len, c)


def _merge_kernel(attn_ref, hy_ref, gate_ref, x_ref, wa_ref, wh_ref, wo_ref, g2_ref,
                  x1_ref, hn_ref):
    d = x_ref.shape[1]
    pa = jnp.dot(attn_ref[...], wa_ref[...], preferred_element_type=F32)
    ph = jnp.dot(hy_ref[...], wh_ref[...], preferred_element_type=F32)
    merged = gate_ref[:, :d].astype(F32) * pa + gate_ref[:, d:].astype(F32) * ph
    x1 = x_ref[...] + jnp.dot(merged.astype(BF16), wo_ref[...], preferred_element_type=F32)
    x1_ref[...] = x1
    ms = jnp.mean(x1 * x1, axis=-1, keepdims=True)
    hn_ref[...] = (x1 * lax.rsqrt(ms + NORM_EPS) * g2_ref[...]).astype(hn_ref.dtype)


def _merge(attn, hyena, gates, x, wa, wh, wo, norm2, tm):
    t, d = x.shape
    const = lambda a: pl.BlockSpec(a.shape, lambda i: (0, 0), pipeline_mode=pl.Buffered(1))
    row = lambda n: pl.BlockSpec((tm, n), lambda i: (i, 0))
    g2 = norm2.reshape(1, d)
    return pl.pallas_call(
        _merge_kernel,
        grid=(t // tm,),
        in_specs=[row(attn.shape[1]), row(hyena.shape[1]), row(gates.shape[1]), row(d),
                  const(wa), const(wh), const(wo), const(g2)],
        out_specs=[row(d), row(d)],
        out_shape=[jax.ShapeDtypeStruct((t, d), F32), jax.ShapeDtypeStruct((t, d), BF16)],
        compiler_params=_cparams("parallel"),
        name="merge_out_proj",
    )(attn, hyena, gates, x, wa, wh, wo, g2)


def _ffn_kernel(prev_ref, main_ref, next_ref, x1_ref, wg_ref, wv_ref, cg_ref, cv_ref, bg_ref, bv_ref,
                wd_ref, gf_ref, o_ref, ext_ref, acc_ref, *, blocks_per_seq):
    i = pl.program_id(0)
    j = pl.program_id(1)
    halo = prev_ref.shape[0]
    tm = main_ref.shape[0]

    @pl.when(j == 0)
    def _():
        first = (i % blocks_per_seq) == 0
        last = (i % blocks_per_seq) == blocks_per_seq - 1
        zero = jnp.zeros(prev_ref.shape, prev_ref.dtype)
        ext_ref[0:halo, :] = jnp.where(first, zero, prev_ref[...])
        ext_ref[halo:halo + tm, :] = main_ref[...]
        ext_ref[halo + tm:, :] = jnp.where(last, zero, next_ref[...])
        acc_ref[...] = jnp.zeros(acc_ref.shape, F32)

    ext = ext_ref[...]
    ug = jnp.dot(ext, wg_ref[...], preferred_element_type=F32)
    uv = jnp.dot(ext, wv_ref[...], preferred_element_type=F32)
    ug = _conv3_rows(ug, cg_ref[...], tm, halo) + bg_ref[...]
    uv = _conv3_rows(uv, cv_ref[...], tm, halo) + bv_ref[...]
    act = (ug / (1.0 + jnp.exp(-ug))) * uv
    acc_ref[...] += jnp.dot(act.astype(BF16), wd_ref[...], preferred_element_type=F32)

    @pl.when(j == pl.num_programs(1) - 1)
    def _():
        y = x1_ref[...] + acc_ref[...]
        ms = jnp.mean(y * y, axis=-1, keepdims=True)
        o_ref[...] = y * lax.rsqrt(ms + NORM_EPS) * gf_ref[...]


def _ffn(hn2, x1, w_up, conv_w, conv_b, w_down, norm_f, seq_len, tm, tf):
    t, d = x1.shape
    dff = w_down.shape[0]
    nj = dff // tf
    halo = BF16_SUBLANES
    r = tm // halo
    nb = t // halo
    kern = functools.partial(_ffn_kernel, blocks_per_seq=seq_len // tm)
    cb = conv_b.reshape(1, 2 * dff)
    return pl.pallas_call(
        kern,
        grid=(t // tm, nj),
        in_specs=[pl.BlockSpec((halo, d), lambda i, j: (jnp.maximum(i * r - 1, 0), 0)),
                  pl.BlockSpec((tm, d), lambda i, j: (i, 0)),
                  pl.BlockSpec((halo, d), lambda i, j: (jnp.minimum((i + 1) * r, nb - 1), 0)),
                  pl.BlockSpec((tm, d), lambda i, j: (i, 0)),
                  pl.BlockSpec((d, tf), lambda i, j: (0, j)),
                  pl.BlockSpec((d, tf), lambda i, j: (0, nj + j)),
                  pl.BlockSpec((3, tf), lambda i, j: (0, j)),
                  pl.BlockSpec((3, tf), lambda i, j: (0, nj + j)),
                  pl.BlockSpec((1, tf), lambda i, j: (0, j)),
                  pl.BlockSpec((1, tf), lambda i, j: (0, nj + j)),
                  pl.BlockSpec((tf, d), lambda i, j: (j, 0)),
                  pl.BlockSpec((1, d), lambda i, j: (0, 0))],
        out_specs=pl.BlockSpec((tm, d), lambda i, j: (i, 0)),
        out_shape=jax.ShapeDtypeStruct((t, d), F32),
        scratch_shapes=[pltpu.VMEM((tm + 2 * halo, d), BF16), pltpu.VMEM((tm, d), F32)],
        compiler_params=_cparams("parallel", "arbitrary"),
        name="conv_glu_mlp",
    )(hn2, hn2, hn2, x1, w_up, w_up, conv_w, conv_w, cb, cb, w_down, norm_f.reshape(1, d))


def _tiles(seq_len, d_model, d_ff):
    return dict(
        tm_norm=_pick(seq_len, 512),
        tm_proj=_pick(seq_len, 1024),
        tn_proj=1024,
        tq=_pick(seq_len, 512),
        tk=_pick(seq_len, 1024),
        tm_prep=_pick(seq_len, 512),
        tl_filt=_pick(seq_len, 512),
        tm2_dft=BF16_SUBLANES,
        tc_dft=256,
        tm_merge=_pick(seq_len, 256),
        tm_ffn=_pick(seq_len, 512),
        tf_ffn=_pick(d_ff, 512),
    )


def _encoder_layer(x, batch, seq_len, lambda_init, p, consts):
    t, d = x.shape
    tl = _tiles(seq_len, d, p["w_down"].shape[0])
    hn = _rmsnorm(x, p["norm1"], tl["tm_norm"])

    qk = _proj_rope(hn, p["w_qk"], consts["cos"], consts["sin"], seq_len, tl["tm_proj"])
    v = _proj_plain(hn, p["w_v"], tl["tm_proj"], _pick(p["w_v"].shape[1], tl["tn_proj"]))
    hy = _proj_plain(hn, p["w_hy"], tl["tm_proj"], _pick(p["w_hy"].shape[1], tl["tn_proj"]))
    gates = _proj_gate(hn, p["w_gl"], p["gate_b"], tl["tm_proj"], _pick(p["w_gl"].shape[1], tl["tn_proj"]))

    attn = _attention(qk.reshape(batch, seq_len, -1), v.reshape(batch, seq_len, -1),
                      p["lambda_q1"], p["lambda_k1"], p["lambda_q2"], p["lambda_k2"], p["subln_g"],
                      batch, seq_len, lambda_init, tl["tq"], tl["tk"]).reshape(t, ATTN_WIDTH)

    u, x0 = _hyena_prep(hy, p["in_conv_w"], p["in_conv_b"], seq_len, tl["tm_prep"])
    circ = _hyena_filters(seq_len, p["filt_w1"], p["filt_b1"], p["filt_w2"], p["filt_b2"],
                          p["filt_w3"], p["filt_b3"], p["filt_w4"], p["filt_freq"], tl["tl_filt"])
    hyena = _hyena_long_conv(u, x0, circ, p["hyena_bias"], batch, seq_len, consts["dft"],
                             tl["tm2_dft"], tl["tc_dft"])

    x1, hn2 = _merge(attn, hyena, gates, x, p["w_attn_out"], p["w_hyena_out"], p["w_out"], p["norm2"],
                     tl["tm_merge"])
    return x1, hn2, tl


def kernel(x_prompt, x_sample, norm1, w_in, in_conv_w, in_conv_b, gate_b, lambda_q1, lambda_k1, lambda_q2, lambda_k2, subln_g, filt_w1, filt_b1, filt_w2, filt_b2, filt_w3, filt_b3, filt_w4, filt_freq, hyena_bias, w_attn_out, w_hyena_out, w_out, norm2, w_up, ffn_conv_w, ffn_conv_b, w_down, norm_f):
    depth = norm1.shape[0]
    assert depth == 1, "this kernel implements the single-layer trunk"
    d_model = x_prompt.shape[-1]
    hyena_width = hyena_bias.shape[-1]
    o1 = ATTN_QK_WIDTH
    o2 = o1 + ATTN_QK_WIDTH
    o3 = o2 + ATTN_WIDTH
    o4 = o3 + 3 * hyena_width

    layers = []
    for i in range(depth):
        wi = w_in[i].astype(BF16)
        layers.append(dict(
            norm1=norm1[i], w_qk=wi[:, :o2], w_v=wi[:, o2:o3], w_hy=wi[:, o3:o4], w_gl=wi[:, o4:],
            in_conv_w=in_conv_w[i], in_conv_b=in_conv_b[i], gate_b=gate_b[i],
            lambda_q1=lambda_q1[i], lambda_k1=lambda_k1[i], lambda_q2=lambda_q2[i], lambda_k2=lambda_k2[i],
            subln_g=subln_g[i],
            filt_w1=filt_w1[i], filt_b1=filt_b1[i], filt_w2=filt_w2[i], filt_b2=filt_b2[i],
            filt_w3=filt_w3[i], filt_b3=filt_b3[i], filt_w4=filt_w4[i], filt_freq=filt_freq[i],
            hyena_bias=hyena_bias[i],
            w_attn_out=w_attn_out[i].astype(BF16), w_hyena_out=w_hyena_out[i].astype(BF16),
            w_out=w_out[i].astype(BF16), norm2=norm2[i],
            w_up=w_up[i].astype(BF16), ffn_conv_w=ffn_conv_w[i], ffn_conv_b=ffn_conv_b[i],
            w_down=w_down[i].astype(BF16)))

    def trunk(x3):
        batch, seq_len, _ = x3.shape
        cos, sin = _rope_tables(seq_len)
        consts = dict(cos=cos, sin=sin, dft=_dft_tables(seq_len))
        x = x3.reshape(batch * seq_len, d_model)
        lambda_init = 0.8 - 0.6 * math.exp(-0.3 * 0)
        p = layers[0]
        x1, hn2, tl = _encoder_layer(x, batch, seq_len, lambda_init, p, consts)
        y = _ffn(hn2, x1, p["w_up"], p["ffn_conv_w"], p["ffn_conv_b"], p["w_down"], norm_f, seq_len,
                 tl["tm_ffn"], tl["tf_ffn"])
        return y.reshape(batch, seq_len, d_model)

    return (trunk(x_prompt), trunk(x_sample))
```

```python
import functools
import math

import numpy as np
import jax
import jax.numpy as jnp
from jax import lax
from jax.experimental import pallas as pl
from jax.experimental.pallas import tpu as pltpu

ATTN_HEADS = 8
ATTN_QK_DIM = 64
ATTN_V_DIM = 2 * ATTN_QK_DIM
ATTN_QK_WIDTH = ATTN_HEADS * 2 * ATTN_QK_DIM
ATTN_WIDTH = ATTN_HEADS * ATTN_V_DIM
HYENA_EMB_DIM = 33
HYENA_BANDS = (HYENA_EMB_DIM - 1) // 2
DECAY_TARGET = 1e-2
FAST_DECAY_PCT = 0.3
SLOW_DECAY_PCT = 1.5
ROPE_THETA = 10000.0
NORM_EPS = 1e-6

LANES = 128
BF16_SUBLANES = 16
VMEM_LIMIT_BYTES = 56 * 1024 * 1024

DFT_INNER = LANES

F32 = jnp.float32
BF16 = jnp.bfloat16


def _cparams(*sem):
    return pltpu.CompilerParams(dimension_semantics=sem, vmem_limit_bytes=VMEM_LIMIT_BYTES)


def _pick(n, pref):
    t = min(n, pref)
    while n % t:
        t //= 2
    return t


def _rmsnorm_kernel(x_ref, g_ref, o_ref):
    x = x_ref[...]
    ms = jnp.mean(x * x, axis=-1, keepdims=True)
    o_ref[...] = (x * lax.rsqrt(ms + NORM_EPS) * g_ref[...]).astype(o_ref.dtype)


def _rmsnorm(x, g, tm):
    t, d = x.shape
    return pl.pallas_call(
        _rmsnorm_kernel,
        grid=(t // tm,),
        in_specs=[pl.BlockSpec((tm, d), lambda i: (i, 0)),
                  pl.BlockSpec((1, d), lambda i: (0, 0))],
        out_specs=pl.BlockSpec((tm, d), lambda i: (i, 0)),
        out_shape=jax.ShapeDtypeStruct((t, d), BF16),
        compiler_params=_cparams("parallel"),
        name="rmsnorm_in",
    )(x, g.reshape(1, d))


def _proj_plain_kernel(h_ref, w_ref, o_ref):
    o_ref[...] = jnp.dot(h_ref[...], w_ref[...], preferred_element_type=F32).astype(o_ref.dtype)


def _proj_plain(h, w, tm, tn):
    t, d = h.shape
    n = w.shape[1]
    return pl.pallas_call(
        _proj_plain_kernel,
        grid=(t // tm, n // tn),
        in_specs=[pl.BlockSpec((tm, d), lambda i, j: (i, 0)),
                  pl.BlockSpec((d, tn), lambda i, j: (0, j))],
        out_specs=pl.BlockSpec((tm, tn), lambda i, j: (i, j)),
        out_shape=jax.ShapeDtypeStruct((t, n), BF16),
        compiler_params=_cparams("parallel", "arbitrary"),
        name="proj_plain",
    )(h, w)


def _proj_gate_kernel(h_ref, w_ref, b_ref, o_ref):
    z = jnp.dot(h_ref[...], w_ref[...], preferred_element_type=F32) + b_ref[...]
    o_ref[...] = (1.0 / (1.0 + jnp.exp(-z))).astype(o_ref.dtype)


def _proj_gate(h, w, b, tm, tn):
    t, d = h.shape
    n = w.shape[1]
    return pl.pallas_call(
        _proj_gate_kernel,
        grid=(t // tm, n // tn),
        in_specs=[pl.BlockSpec((tm, d), lambda i, j: (i, 0)),
                  pl.BlockSpec((d, tn), lambda i, j: (0, j)),
                  pl.BlockSpec((1, tn), lambda i, j: (0, j))],
        out_specs=pl.BlockSpec((tm, tn), lambda i, j: (i, j)),
        out_shape=jax.ShapeDtypeStruct((t, n), BF16),
        compiler_params=_cparams("parallel", "arbitrary"),
        name="proj_gate",
    )(h, w, b.reshape(1, n))


def _proj_rope_kernel(h_ref, w_ref, cos_ref, sin_ref, o_ref, *, q_scale):
    acc = jnp.dot(h_ref[...], w_ref[...], preferred_element_type=F32)
    cos = cos_ref[...]
    sin = sin_ref[...]
    lane = lax.broadcasted_iota(jnp.int32, cos.shape, 1)
    first_half = (lane % ATTN_QK_DIM) < (ATTN_QK_DIM // 2)
    scale = jnp.where(pl.program_id(1) == 0, q_scale, 1.0).astype(F32)
    half = ATTN_QK_DIM // 2
    for g in range(acc.shape[1] // LANES):
        x = acc[:, g * LANES:(g + 1) * LANES]
        up = pltpu.roll(x, LANES - half, 1)
        down = pltpu.roll(x, half, 1)
        rot = jnp.where(first_half, up, down)
        o_ref[:, g * LANES:(g + 1) * LANES] = ((x * cos + rot * sin) * scale).astype(o_ref.dtype)


def _proj_rope(h, w, cos, sin, seq_len, tm):
    t, d = h.shape
    n = w.shape[1]
    tn = ATTN_QK_WIDTH
    nblk = seq_len // tm
    kern = functools.partial(_proj_rope_kernel, q_scale=ATTN_QK_DIM ** -0.5 * math.log2(math.e))
    return pl.pallas_call(
        kern,
        grid=(t // tm, n // tn),
        in_specs=[pl.BlockSpec((tm, d), lambda i, j: (i, 0)),
                  pl.BlockSpec((d, tn), lambda i, j: (0, j)),
                  pl.BlockSpec((tm, LANES), lambda i, j: (i % nblk, 0)),
                  pl.BlockSpec((tm, LANES), lambda i, j: (i % nblk, 0))],
        out_specs=pl.BlockSpec((tm, tn), lambda i, j: (i, j)),
        out_shape=jax.ShapeDtypeStruct((t, n), BF16),
        compiler_params=_cparams("parallel", "arbitrary"),
        name="proj_rope",
    )(h, w, cos, sin)


def _rope_tables(seq_len):
    d = ATTN_QK_DIM
    inv = ROPE_THETA ** (-jnp.arange(0, d, 2, dtype=F32) / d)
    ang = jnp.arange(seq_len, dtype=F32)[:, None] * inv[None]
    ang = jnp.concatenate([ang, ang], -1)
    sign = jnp.concatenate([-jnp.ones((d // 2,), F32), jnp.ones((d // 2,), F32)])
    cos = jnp.cos(ang)
    sin = jnp.sin(ang) * sign
    reps = LANES // d
    return jnp.tile(cos, (1, reps)), jnp.tile(sin, (1, reps))


def _attn_kernel(q_ref, k_ref, v_ref, lq1_ref, lk1_ref, lq2_ref, lk2_ref, g_ref, o_ref,
                 sa0_ref, sb0_ref, sa1_ref, sb1_ref, m0_ref, m1_ref, acc0_ref, acc1_ref,
                 *, tq, tk, lambda_init):
    seq = k_ref.shape[0]
    n_chunks = seq // tk
    n_tiles = seq // tq
    lane = lax.broadcasted_iota(jnp.int32, (tq, LANES), 1)
    ones_col = (lax.broadcasted_iota(jnp.int32, (tk, LANES), 1) == 0).astype(BF16)
    nt = (((1,), (1,)), ((), ()))
    score_refs = ((sa0_ref, sb0_ref), (sa1_ref, sb1_ref))
    lam = (jnp.exp(jnp.sum(lq1_ref[...] * lk1_ref[...], axis=1, keepdims=True))
           - jnp.exp(jnp.sum(lq2_ref[...] * lk2_ref[...], axis=1, keepdims=True))
           + lambda_init)

    def tile_rows(tile):
        start = tile * tq if isinstance(tile, int) else pl.multiple_of(tile * tq, tq)
        return pl.ds(start, tq)

    def split_maps(tile):
        q = q_ref[tile_rows(tile), :]
        zero = jnp.zeros_like(q)
        return jnp.where(lane < ATTN_QK_DIM, q, zero), jnp.where(lane >= ATTN_QK_DIM, q, zero)

    def scores(qmaps, j, slot):
        kc = k_ref[j * tk:(j + 1) * tk, :]
        sa_ref, sb_ref = score_refs[slot]
        sa_ref[...] = lax.dot_general(qmaps[0], kc, nt, preferred_element_type=F32)
        sb_ref[...] = lax.dot_general(qmaps[1], kc, nt, preferred_element_type=F32)

    def one_map(s_ref, vext, m_ref, acc_ref, first):
        s = s_ref[...]
        m_cur = jnp.max(s, axis=1, keepdims=True)
        if first:
            m_new = jnp.broadcast_to(m_cur, m_ref.shape)
        else:
            m_old = m_ref[...]
            m_new = jnp.maximum(m_old, m_cur)
        p = jnp.concatenate(
            [jnp.exp2(s[:, c * LANES:(c + 1) * LANES] - m_new) for c in range(tk // LANES)],
            axis=1).astype(BF16)
        pv = jnp.dot(p, vext, preferred_element_type=F32)
        if first:
            acc_ref[...] = pv
        else:
            alpha = jnp.exp2(m_old - m_new)
            acc_ref[...] = jnp.concatenate([alpha, alpha], axis=1) * acc_ref[...] + pv
        m_ref[...] = m_new

    def weighted_values(j, slot):
        vext = jnp.concatenate([v_ref[j * tk:(j + 1) * tk, :], ones_col], axis=1)
        sa_ref, sb_ref = score_refs[slot]
        one_map(sa_ref, vext, m0_ref, acc0_ref, j == 0)
        one_map(sb_ref, vext, m1_ref, acc1_ref, j == 0)

    def finish(tile):
        a0 = acc0_ref[...]
        a1 = acc1_ref[...]
        o0 = a0[:, :LANES] / a0[:, LANES:LANES + 1]
        o1 = a1[:, :LANES] / a1[:, LANES:LANES + 1]
        o = o0 - lam * o1
        ms = jnp.mean(o * o, axis=-1, keepdims=True)
        o = o * lax.rsqrt(ms + NORM_EPS) * g_ref[...]
        o_ref[tile_rows(tile), :] = (o * (1.0 - lambda_init)).astype(o_ref.dtype)

    scores(split_maps(0), 0, 0)
    acc0_ref[...] = jnp.ones(acc0_ref.shape, F32)
    acc1_ref[...] = jnp.ones(acc1_ref.shape, F32)

    def tile_body(tile, carry):
        qmaps = split_maps(tile)
        for j in range(0, n_chunks, 2):
            scores(qmaps, j + 1, 1)
            if j == 0:
                finish(jnp.maximum(tile - 1, 0))
            weighted_values(j, 0)
            if j + 2 < n_chunks:
                scores(qmaps, j + 2, 0)
            else:
                scores(split_maps(jnp.minimum(tile + 1, n_tiles - 1)), 0, 0)
            weighted_values(j + 1, 1)
        return carry

    lax.fori_loop(0, n_tiles, tile_body, 0)
    finish(n_tiles - 1)


def _attention(qk, v, lq1, lk1, lq2, lk2, subln_g, batch, seq_len, lambda_init, tq, tk):
    assert (seq_len // tk) % 2 == 0, "key chunks are processed in pairs"
    kern = functools.partial(_attn_kernel, tq=tq, tk=tk, lambda_init=lambda_init)
    small = pl.BlockSpec((1, ATTN_QK_DIM), lambda b, h: (0, 0))
    return pl.pallas_call(
        kern,
        grid=(batch, ATTN_HEADS),
        in_specs=[pl.BlockSpec((None, seq_len, LANES), lambda b, h: (b, 0, h)),
                  pl.BlockSpec((None, seq_len, LANES), lambda b, h: (b, 0, ATTN_HEADS + h)),
                  pl.BlockSpec((None, seq_len, LANES), lambda b, h: (b, 0, h)),
                  small, small, small, small,
                  pl.BlockSpec((1, LANES), lambda b, h: (0, 0))],
        out_specs=pl.BlockSpec((None, seq_len, LANES), lambda b, h: (b, 0, h)),
        out_shape=jax.ShapeDtypeStruct((batch, seq_len, ATTN_WIDTH), BF16),
        scratch_shapes=[pltpu.VMEM((tq, tk), F32)] * 4
                       + [pltpu.VMEM((tq, LANES), F32), pltpu.VMEM((tq, LANES), F32),
                          pltpu.VMEM((tq, 2 * LANES), F32), pltpu.VMEM((tq, 2 * LANES), F32)],
        compiler_params=_cparams("parallel", "parallel"),
        name="diff_attention",
    )(qk, qk, v, lq1.reshape(1, -1), lk1.reshape(1, -1), lq2.reshape(1, -1), lk2.reshape(1, -1),
      subln_g.reshape(1, -1))


def _conv3_rows(e, w, n_rows, halo):
    tot = e.shape[0]
    prev = pltpu.roll(e, 1, 0)
    nxt = pltpu.roll(e, tot - 1, 0)
    c = prev * w[0:1, :] + e * w[1:2, :] + nxt * w[2:3, :]
    return c[halo:halo + n_rows, :]


def _hyena_prep_kernel(prev_ref, main_ref, next_ref, w_ref, b_ref, u_ref, x0_ref, *, blocks_per_seq):
    i = pl.program_id(0)
    first = (i % blocks_per_seq) == 0
    last = (i % blocks_per_seq) == blocks_per_seq - 1
    halo = prev_ref.shape[0]
    tm = main_ref.shape[0]
    prev = jnp.where(first, 0.0, prev_ref[...].astype(F32))
    nxt = jnp.where(last, 0.0, next_ref[...].astype(F32))
    e = jnp.concatenate([prev, main_ref[...].astype(F32), nxt], axis=0)
    c = _conv3_rows(e, w_ref[...], tm, halo) + b_ref[...]
    width = u_ref.shape[1]
    x0 = c[:, :width]
    x1 = c[:, width:2 * width]
    hv = c[:, 2 * width:]
    u_ref[...] = (hv * x1).astype(u_ref.dtype)
    x0_ref[...] = x0.astype(x0_ref.dtype)


def _hyena_prep(hy, conv_w, conv_b, seq_len, tm):
    t, n = hy.shape
    width = n // 3
    halo = BF16_SUBLANES
    r = tm // halo
    nb = t // halo
    kern = functools.partial(_hyena_prep_kernel, blocks_per_seq=seq_len // tm)
    return pl.pallas_call(
        kern,
        grid=(t // tm,),
        in_specs=[pl.BlockSpec((halo, n), lambda i: (jnp.maximum(i * r - 1, 0), 0)),
                  pl.BlockSpec((tm, n), lambda i: (i, 0)),
                  pl.BlockSpec((halo, n), lambda i: (jnp.minimum((i + 1) * r, nb - 1), 0)),
                  pl.BlockSpec((3, n), lambda i: (0, 0)),
                  pl.BlockSpec((1, n), lambda i: (0, 0))],
        out_specs=[pl.BlockSpec((tm, width), lambda i: (i, 0)),
                   pl.BlockSpec((tm, width), lambda i: (i, 0))],
        out_shape=[jax.ShapeDtypeStruct((t, width), BF16),
                   jax.ShapeDtypeStruct((t, width), BF16)],
        compiler_params=_cparams("parallel"),
        name="hyena_prep",
    )(hy, hy, hy, conv_w, conv_b.reshape(1, n))


def _filter_kernel(z_ref, w1_ref, b1_ref, w2_ref, b2_ref, w3_ref, b3_ref, w4_ref, f_ref, d_ref, o_ref,
                   *, seq_len):
    hp = lax.Precision.HIGHEST
    z = z_ref[...]
    f = f_ref[...]
    h = jnp.sin(f * (jnp.dot(z, w1_ref[...], precision=hp, preferred_element_type=F32) + b1_ref[...]))
    h = jnp.sin(f * (jnp.dot(h, w2_ref[...], precision=hp, preferred_element_type=F32) + b2_ref[...]))
    h = jnp.sin(f * (jnp.dot(h, w3_ref[...], precision=hp, preferred_element_type=F32) + b3_ref[...]))
    h = jnp.dot(h, w4_ref[...], precision=hp, preferred_element_type=F32)
    t = z[:, 0:1]
    h = h * jnp.exp(-t * d_ref[...])
    row = pl.program_id(0) * z.shape[0] + lax.broadcasted_iota(jnp.int32, (z.shape[0], 1), 0)
    o_ref[...] = jnp.where(row == seq_len, 0.0, h).astype(o_ref.dtype)


def _hyena_filters(seq_len, w1, b1, w2, b2, w3, b3, w4, freq, tl):
    fw = w1.shape[1]
    c = w4.shape[1] // 2
    t = jnp.linspace(0.0, 1.0, seq_len, dtype=F32)[:, None]
    w = 2.0 * math.pi * jnp.arange(seq_len, dtype=F32)[:, None] / seq_len
    f = jnp.linspace(1e-4, HYENA_BANDS - 1, HYENA_BANDS, dtype=F32)[None]
    z = jnp.concatenate([t, jnp.cos(f * w), -jnp.sin(f * w)], -1)
    emb = z.shape[1]
    z = jnp.concatenate([z, jnp.zeros((1, emb), F32), z[1:][::-1]], axis=0)
    z = jnp.pad(z, ((0, 0), (0, LANES - emb)))
    w1p = jnp.pad(w1, ((0, LANES - emb), (0, 0)))
    max_decay = math.log(DECAY_TARGET) / FAST_DECAY_PCT
    min_decay = math.log(DECAY_TARGET) / SLOW_DECAY_PCT
    deltas = jnp.abs(jnp.linspace(min_decay, max_decay, c, dtype=F32)).reshape(1, c)
    full = lambda a: pl.BlockSpec(a.shape, lambda i: (0,) * a.ndim)
    half_blocks = seq_len // tl
    pre = [w1p, b1.reshape(1, fw), w2, b2.reshape(1, fw), w3, b3.reshape(1, fw)]
    post = [freq.reshape(1, fw), deltas]
    kern = functools.partial(_filter_kernel, seq_len=seq_len)
    return pl.pallas_call(
        kern,
        grid=(2 * half_blocks,),
        in_specs=([pl.BlockSpec((tl, LANES), lambda i: (i, 0))] + [full(a) for a in pre]
                  + [pl.BlockSpec((fw, c), lambda i: (0, i // half_blocks))]
                  + [full(a) for a in post]),
        out_specs=pl.BlockSpec((tl, c), lambda i: (i, 0)),
        out_shape=jax.ShapeDtypeStruct((2 * seq_len, c), BF16),
        compiler_params=_cparams("parallel"),
        name="hyena_filters",
    )(z, *pre, w4, *post)


def _dft_outer_kernel(w_ref, x_ref, or_ref, oi_ref, xs_ref, os_ref):
    k, tm2, tc = x_ref.shape
    n1 = or_ref.shape[0]
    groups = tc // LANES
    x = x_ref[...].astype(F32).reshape(k * tm2, tc)
    for g in range(groups):
        xs_ref[g] = x[:, g * LANES:(g + 1) * LANES]
    w = w_ref[...]
    for j in range(tm2):
        xj = jnp.concatenate([xs_ref[g, pl.ds(j, k, stride=tm2), :] for g in range(groups)], axis=1)
        res = jnp.dot(w, xj.astype(BF16), preferred_element_type=F32)
        for g in range(groups):
            os_ref[g, pl.ds(j, 2 * n1, stride=tm2), :] = res[:, g * LANES:(g + 1) * LANES]
    for g in range(groups):
        o = os_ref[g].astype(or_ref.dtype).reshape(2 * n1, tm2, LANES)
        or_ref[:, :, g * LANES:(g + 1) * LANES] = o[:n1]
        oi_ref[:, :, g * LANES:(g + 1) * LANES] = o[n1:]


def _dft_outer(w, x4, tm2, tc):
    g, k, n2, c = x4.shape
    n1 = w.shape[0] // 2
    out = jax.ShapeDtypeStruct((g, n1, n2, c), BF16)
    return pl.pallas_call(
        _dft_outer_kernel,
        grid=(g, n2 // tm2, c // tc),
        in_specs=[pl.BlockSpec(w.shape, lambda a, i, j: (0, 0)),
                  pl.BlockSpec((None, k, tm2, tc), lambda a, i, j: (a, 0, i, j))],
        out_specs=[pl.BlockSpec((None, n1, tm2, tc), lambda a, i, j: (a, 0, i, j)),
                   pl.BlockSpec((None, n1, tm2, tc), lambda a, i, j: (a, 0, i, j))],
        out_shape=[out, out],
        scratch_shapes=[pltpu.VMEM((tc // LANES, k * tm2, LANES), F32),
                        pltpu.VMEM((tc // LANES, 2 * n1 * tm2, LANES), F32)],
        compiler_params=_cparams("parallel", "parallel", "parallel"),
        name="dft_outer",
    )(w, x4)


def _spectrum_kernel(ar_ref, ai_ref, wf_ref, o_ref, s_ref, *, scale):
    n2 = ar_ref.shape[0]
    s_ref[0:n2, :] = ar_ref[...]
    s_ref[n2:2 * n2, :] = ai_ref[...]
    o_ref[...] = jnp.dot(wf_ref[...], s_ref[...], preferred_element_type=F32) * scale


def _filter_spectrum(ar, ai, wf, scale):
    n1, n2, c = ar.shape
    r = 2 * n2
    kern = functools.partial(_spectrum_kernel, scale=scale)
    slab = pl.BlockSpec((None, n2, c), lambda k: (k, 0, 0))
    return pl.pallas_call(
        kern,
        grid=(n1,),
        in_specs=[slab, slab, pl.BlockSpec((None, r, r), lambda k: (k, 0, 0))],
        out_specs=pl.BlockSpec((None, r, c), lambda k: (k, 0, 0)),
        out_shape=jax.ShapeDtypeStruct((n1, r, c), F32),
        scratch_shapes=[pltpu.VMEM((r, c), BF16)],
        compiler_params=_cparams("parallel"),
        name="filter_spectrum",
    )(ar, ai, wf)


def _fft_mid_kernel(ar_ref, ai_ref, h_ref, wf_ref, wi_ref, zr_ref, zi_ref, s_ref):
    n2 = ar_ref.shape[0]
    s_ref[0:n2, :] = ar_ref[...]
    s_ref[n2:2 * n2, :] = ai_ref[...]
    x = jnp.dot(wf_ref[...], s_ref[...], preferred_element_type=F32)
    h = h_ref[...]
    xr, xi = x[:n2], x[n2:]
    hr, hi = h[:n2], h[n2:]
    s_ref[0:n2, :] = (xr * hr - xi * hi).astype(s_ref.dtype)
    s_ref[n2:2 * n2, :] = (xr * hi + xi * hr).astype(s_ref.dtype)
    z = jnp.dot(wi_ref[...], s_ref[...], preferred_element_type=F32)
    zr_ref[...] = z[:n2].astype(zr_ref.dtype)
    zi_ref[...] = z[n2:].astype(zi_ref.dtype)


def _fft_mid(ar, ai, hspec, wf, wi):
    batch, n1, n2, c = ar.shape
    r = 2 * n2
    slab = pl.BlockSpec((None, None, n2, c), lambda k, b: (b, k, 0, 0))
    out = jax.ShapeDtypeStruct((batch, n1, n2, c), BF16)
    return pl.pallas_call(
        _fft_mid_kernel,
        grid=(n1, batch),
        in_specs=[slab, slab,
                  pl.BlockSpec((None, r, c), lambda k, b: (k, 0, 0)),
                  pl.BlockSpec((None, r, r), lambda k, b: (k, 0, 0)),
                  pl.BlockSpec((None, r, r), lambda k, b: (k, 0, 0))],
        out_specs=[slab, slab],
        out_shape=[out, out],
        scratch_shapes=[pltpu.VMEM((r, c), BF16)],
        compiler_params=_cparams("parallel", "arbitrary"),
        name="fft_mid",
    )(ar, ai, hspec, wf, wi)


def _fft_out_kernel(w_ref, zr_ref, zi_ref, u_ref, x0_ref, b_ref, o_ref, rs_ref, is_ref, ys_ref):
    n1, tm2, tc = zr_ref.shape
    n1h = o_ref.shape[0]
    groups = tc // LANES
    zr = zr_ref[...].astype(F32).reshape(n1 * tm2, tc)
    zi = zi_ref[...].astype(F32).reshape(n1 * tm2, tc)
    for g in range(groups):
        rs_ref[g] = zr[:, g * LANES:(g + 1) * LANES]
        is_ref[g] = zi[:, g * LANES:(g + 1) * LANES]
    w = w_ref[...]
    for j in range(tm2):
        parts = [jnp.concatenate([rs_ref[g, pl.ds(j, n1, stride=tm2), :] for g in range(groups)], axis=1),
                 jnp.concatenate([is_ref[g, pl.ds(j, n1, stride=tm2), :] for g in range(groups)], axis=1)]
        if w.shape[1] > 2 * n1:
            parts.append(jnp.zeros((w.shape[1] - 2 * n1, tc), F32))
        zj = jnp.concatenate(parts, axis=0)
        y = jnp.dot(w, zj.astype(BF16), preferred_element_type=F32)
        for g in range(groups):
            ys_ref[g, pl.ds(j, n1h, stride=tm2), :] = y[:, g * LANES:(g + 1) * LANES]
    for g in range(groups):
        sl = slice(g * LANES, (g + 1) * LANES)
        y = ys_ref[g].reshape(n1h, tm2, LANES)
        u = u_ref[:, :, sl].astype(F32)
        o_ref[:, :, sl] = ((y + u * b_ref[:, sl]) * x0_ref[:, :, sl].astype(F32)).astype(o_ref.dtype)


def _fft_out(w, zr, zi, u4, x04, bias, tm2, tc):
    batch, n1, n2, c = zr.shape
    n1h = w.shape[0]
    zspec = pl.BlockSpec((None, n1, tm2, tc), lambda a, i, j: (a, 0, i, j))
    uspec = pl.BlockSpec((None, n1h, tm2, tc), lambda a, i, j: (a, 0, i, j))
    return pl.pallas_call(
        _fft_out_kernel,
        grid=(batch, n2 // tm2, c // tc),
        in_specs=[pl.BlockSpec(w.shape, lambda a, i, j: (0, 0)), zspec, zspec, uspec, uspec,
                  pl.BlockSpec((1, tc), lambda a, i, j: (0, j))],
        out_specs=uspec,
        out_shape=jax.ShapeDtypeStruct((batch, n1h, n2, c), BF16),
        scratch_shapes=[pltpu.VMEM((tc // LANES, n1 * tm2, LANES), F32),
                        pltpu.VMEM((tc // LANES, n1 * tm2, LANES), F32),
                        pltpu.VMEM((tc // LANES, n1h * tm2, LANES), F32)],
        compiler_params=_cparams("parallel", "parallel", "parallel"),
        name="fft_out",
    )(w, zr, zi, u4, x04, bias.reshape(1, c))


def _dft_tables(seq_len):
    n = 2 * seq_len
    n2 = DFT_INNER
    n1 = n // n2
    n1h = n1 // 2
    nk = n1h + 8
    k1 = np.arange(nk)[:, None]
    m1 = np.arange(n1)[None, :]
    ang1 = 2.0 * np.pi * k1 * m1 / n1
    w1_full = np.concatenate([np.cos(ang1), -np.sin(ang1)], axis=0)
    w1_half = w1_full[:, :n1h]
    weight = np.where((np.arange(nk) == 0) | (np.arange(nk) == n1h), 1.0, 2.0) * (np.arange(nk) <= n1h)
    ang1i = ang1.T[:n1h]
    w1_inv = np.concatenate([np.cos(ang1i) * weight, -np.sin(ang1i) * weight], axis=1)
    w1_inv = np.pad(w1_inv, ((0, 0), (0, -(2 * nk) % LANES)))
    k1d = jnp.arange(nk, dtype=jnp.int32)[:, None, None]
    k2d = jnp.arange(n2, dtype=jnp.int32)[None, :, None]
    m2d = jnp.arange(n2, dtype=jnp.int32)[None, None, :]
    phase = (m2d * (k1d + n1 * k2d)) % n
    ang = phase.astype(F32) * (2.0 * math.pi / n)
    c, s = jnp.cos(ang), jnp.sin(ang)
    w2_fwd = jnp.concatenate([jnp.concatenate([c, s], axis=2), jnp.concatenate([-s, c], axis=2)], axis=1)
    ct, st = jnp.swapaxes(c, 1, 2), jnp.swapaxes(s, 1, 2)
    w2_inv = jnp.concatenate([jnp.concatenate([ct, -st], axis=2), jnp.concatenate([st, ct], axis=2)], axis=1)
    bf = lambda a: jnp.asarray(a, F32).astype(BF16)
    return dict(n1=n1, n2=n2, n1h=n1h, w1_full=bf(w1_full), w1_half=bf(w1_half), w1_inv=bf(w1_inv),
                w2_fwd=w2_fwd.astype(BF16), w2_inv=w2_inv.astype(BF16))


def _hyena_long_conv(u, x0, circ, bias, batch, seq_len, tab, tm2, tc):
    c = u.shape[1]
    n1, n2, n1h = tab["n1"], tab["n2"], tab["n1h"]
    tc = _pick(c, tc)
    far, fai = _dft_outer(tab["w1_full"], circ.reshape(1, n1, n2, c), tm2, tc)
    hspec = _filter_spectrum(far[0], fai[0], tab["w2_fwd"], 1.0 / (n1 * n2))
    u4 = u.reshape(batch, n1h, n2, c)
    x04 = x0.reshape(batch, n1h, n2, c)
    ar, ai = _dft_outer(tab["w1_half"], u4, tm2, tc)
    zr, zi = _fft_mid(ar, ai, hspec, tab["w2_fwd"], tab["w2_inv"])
    y = _fft_out(tab["w1_inv"], zr, zi, u4, x04, bias, tm2, tc)
    return y.reshape(batch * seq_len, c)


def _merge_kernel(attn_ref, hy_ref, gate_ref, x_ref, wa_ref, wh_ref, wo_ref, g2_ref,
                  x1_ref, hn_ref):
    d = x_ref.shape[1]
    pa = jnp.dot(attn_ref[...], wa_ref[...], preferred_element_type=F32)
    ph = jnp.dot(hy_ref[...], wh_ref[...], preferred_element_type=F32)
    merged = gate_ref[:, :d].astype(F32) * pa + gate_ref[:, d:].astype(F32) * ph
    x1 = x_ref[...] + jnp.dot(merged.astype(BF16), wo_ref[...], preferred_element_type=F32)
    x1_ref[...] = x1
    ms = jnp.mean(x1 * x1, axis=-1, keepdims=True)
    hn_ref[...] = (x1 * lax.rsqrt(ms + NORM_EPS) * g2_ref[...]).astype(hn_ref.dtype)


def _merge(attn, hyena, gates, x, wa, wh, wo, norm2, tm):
    t, d = x.shape
    const = lambda a: pl.BlockSpec(a.shape, lambda i: (0, 0), pipeline_mode=pl.Buffered(1))
    row = lambda n: pl.BlockSpec((tm, n), lambda i: (i, 0))
    g2 = norm2.reshape(1, d)
    return pl.pallas_call(
        _merge_kernel,
        grid=(t // tm,),
        in_specs=[row(attn.shape[1]), row(hyena.shape[1]), row(gates.shape[1]), row(d),
                  const(wa), const(wh), const(wo), const(g2)],
        out_specs=[row(d), row(d)],
        out_shape=[jax.ShapeDtypeStruct((t, d), F32), jax.ShapeDtypeStruct((t, d), BF16)],
        compiler_params=_cparams("parallel"),
        name="merge_out_proj",
    )(attn, hyena, gates, x, wa, wh, wo, g2)


def _ffn_kernel(prev_ref, main_ref, next_ref, x1_ref, wg_ref, wv_ref, cg_ref, cv_ref, bg_ref, bv_ref,
                wd_ref, gf_ref, o_ref, ext_ref, acc_ref, *, blocks_per_seq):
    i = pl.program_id(0)
    j = pl.program_id(1)
    halo = prev_ref.shape[0]
    tm = main_ref.shape[0]

    @pl.when(j == 0)
    def _():
        first = (i % blocks_per_seq) == 0
        last = (i % blocks_per_seq) == blocks_per_seq - 1
        zero = jnp.zeros(prev_ref.shape, prev_ref.dtype)
        ext_ref[0:halo, :] = jnp.where(first, zero, prev_ref[...])
        ext_ref[halo:halo + tm, :] = main_ref[...]
        ext_ref[halo + tm:, :] = jnp.where(last, zero, next_ref[...])
        acc_ref[...] = jnp.zeros(acc_ref.shape, F32)

    ext = ext_ref[...]
    ug = jnp.dot(ext, wg_ref[...], preferred_element_type=F32)
    uv = jnp.dot(ext, wv_ref[...], preferred_element_type=F32)
    ug = _conv3_rows(ug, cg_ref[...], tm, halo) + bg_ref[...]
    uv = _conv3_rows(uv, cv_ref[...], tm, halo) + bv_ref[...]
    act = (ug / (1.0 + jnp.exp(-ug))) * uv
    acc_ref[...] += jnp.dot(act.astype(BF16), wd_ref[...], preferred_element_type=F32)

    @pl.when(j == pl.num_programs(1) - 1)
    def _():
        y = x1_ref[...] + acc_ref[...]
        ms = jnp.mean(y * y, axis=-1, keepdims=True)
        o_ref[...] = y * lax.rsqrt(ms + NORM_EPS) * gf_ref[...]


def _ffn(hn2, x1, w_up, conv_w, conv_b, w_down, norm_f, seq_len, tm, tf):
    t, d = x1.shape
    dff = w_down.shape[0]
    nj = dff // tf
    halo = BF16_SUBLANES
    r = tm // halo
    nb = t // halo
    kern = functools.partial(_ffn_kernel, blocks_per_seq=seq_len // tm)
    cb = conv_b.reshape(1, 2 * dff)
    return pl.pallas_call(
        kern,
        grid=(t // tm, nj),
        in_specs=[pl.BlockSpec((halo, d), lambda i, j: (jnp.maximum(i * r - 1, 0), 0)),
                  pl.BlockSpec((tm, d), lambda i, j: (i, 0)),
                  pl.BlockSpec((halo, d), lambda i, j: (jnp.minimum((i + 1) * r, nb - 1), 0)),
                  pl.BlockSpec((tm, d), lambda i, j: (i, 0)),
                  pl.BlockSpec((d, tf), lambda i, j: (0, j)),
                  pl.BlockSpec((d, tf), lambda i, j: (0, nj + j)),
                  pl.BlockSpec((3, tf), lambda i, j: (0, j)),
                  pl.BlockSpec((3, tf), lambda i, j: (0, nj + j)),
                  pl.BlockSpec((1, tf), lambda i, j: (0, j)),
                  pl.BlockSpec((1, tf), lambda i, j: (0, nj + j)),
                  pl.BlockSpec((tf, d), lambda i, j: (j, 0)),
                  pl.BlockSpec((1, d), lambda i, j: (0, 0))],
        out_specs=pl.BlockSpec((tm, d), lambda i, j: (i, 0)),
        out_shape=jax.ShapeDtypeStruct((t, d), F32),
        scratch_shapes=[pltpu.VMEM((tm + 2 * halo, d), BF16), pltpu.VMEM((tm, d), F32)],
        compiler_params=_cparams("parallel", "arbitrary"),
        name="conv_glu_mlp",
    )(hn2, hn2, hn2, x1, w_up, w_up, conv_w, conv_w, cb, cb, w_down, norm_f.reshape(1, d))


def _tiles(seq_len, d_model, d_ff):
    return dict(
        tm_norm=_pick(seq_len, 512),
        tm_proj=_pick(seq_len, 1024),
        tn_proj=1024,
        tq=_pick(seq_len, 512),
        tk=_pick(seq_len, 1024),
        tm_prep=_pick(seq_len, 512),
        tl_filt=_pick(seq_len, 512),
        tm2_dft=BF16_SUBLANES,
        tc_dft=256,
        tm_merge=_pick(seq_len, 256),
        tm_ffn=_pick(seq_len, 512),
        tf_ffn=_pick(d_ff, 512),
    )


def _encoder_layer(x, batch, seq_len, lambda_init, p, consts):
    t, d = x.shape
    tl = _tiles(seq_len, d, p["w_down"].shape[0])
    hn = _rmsnorm(x, p["norm1"], tl["tm_norm"])

    qk = _proj_rope(hn, p["w_qk"], consts["cos"], consts["sin"], seq_len, tl["tm_proj"])
    v = _proj_plain(hn, p["w_v"], tl["tm_proj"], _pick(p["w_v"].shape[1], tl["tn_proj"]))
    hy = _proj_plain(hn, p["w_hy"], tl["tm_proj"], _pick(p["w_hy"].shape[1], tl["tn_proj"]))
    gates = _proj_gate(hn, p["w_gl"], p["gate_b"], tl["tm_proj"], _pick(p["w_gl"].shape[1], tl["tn_proj"]))

    attn = _attention(qk.reshape(batch, seq_len, -1), v.reshape(batch, seq_len, -1),
                      p["lambda_q1"], p["lambda_k1"], p["lambda_q2"], p["lambda_k2"], p["subln_g"],
                      batch, seq_len, lambda_init, tl["tq"], tl["tk"]).reshape(t, ATTN_WIDTH)

    u, x0 = _hyena_prep(hy, p["in_conv_w"], p["in_conv_b"], seq_len, tl["tm_prep"])
    circ = _hyena_filters(seq_len, p["filt_w1"], p["filt_b1"], p["filt_w2"], p["filt_b2"],
                          p["filt_w3"], p["filt_b3"], p["filt_w4"], p["filt_freq"], tl["tl_filt"])
    hyena = _hyena_long_conv(u, x0, circ, p["hyena_bias"], batch, seq_len, consts["dft"],
                             tl["tm2_dft"], tl["tc_dft"])

    x1, hn2 = _merge(attn, hyena, gates, x, p["w_attn_out"], p["w_hyena_out"], p["w_out"], p["norm2"],
                     tl["tm_merge"])
    return x1, hn2, tl


def kernel(x_prompt, x_sample, norm1, w_in, in_conv_w, in_conv_b, gate_b, lambda_q1, lambda_k1, lambda_q2, lambda_k2, subln_g, filt_w1, filt_b1, filt_w2, filt_b2, filt_w3, filt_b3, filt_w4, filt_freq, hyena_bias, w_attn_out, w_hyena_out, w_out, norm2, w_up, ffn_conv_w, ffn_conv_b, w_down, norm_f):
    depth = norm1.shape[0]
    assert depth == 1, "this kernel implements the single-layer trunk"
    d_model = x_prompt.shape[-1]
    hyena_width = hyena_bias.shape[-1]
    o1 = ATTN_QK_WIDTH
    o2 = o1 + ATTN_QK_WIDTH
    o3 = o2 + ATTN_WIDTH
    o4 = o3 + 3 * hyena_width

    layers = []
    for i in range(depth):
        wi = w_in[i].astype(BF16)
        layers.append(dict(
            norm1=norm1[i], w_qk=wi[:, :o2], w_v=wi[:, o2:o3], w_hy=wi[:, o3:o4], w_gl=wi[:, o4:],
            in_conv_w=in_conv_w[i], in_conv_b=in_conv_b[i], gate_b=gate_b[i],
            lambda_q1=lambda_q1[i], lambda_k1=lambda_k1[i], lambda_q2=lambda_q2[i], lambda_k2=lambda_k2[i],
            subln_g=subln_g[i],
            filt_w1=filt_w1[i], filt_b1=filt_b1[i], filt_w2=filt_w2[i], filt_b2=filt_b2[i],
            filt_w3=filt_w3[i], filt_b3=filt_b3[i], filt_w4=filt_w4[i], filt_freq=filt_freq[i],
            hyena_bias=hyena_bias[i],
            w_attn_out=w_attn_out[i].astype(BF16), w_hyena_out=w_hyena_out[i].astype(BF16),
            w_out=w_out[i].astype(BF16), norm2=norm2[i],
            w_up=w_up[i].astype(BF16), ffn_conv_w=ffn_conv_w[i], ffn_conv_b=ffn_conv_b[i],
            w_down=w_down[i].astype(BF16)))

    def trunk(x3):
        batch, seq_len, _ = x3.shape
        cos, sin = _rope_tables(seq_len)
        consts = dict(cos=cos, sin=sin, dft=_dft_tables(seq_len))
        x = x3.reshape(batch * seq_len, d_model)
        lambda_init = 0.8 - 0.6 * math.exp(-0.3 * 0)
        p = layers[0]
        x1, hn2, tl = _encoder_layer(x, batch, seq_len, lambda_init, p, consts)
        y = _ffn(hn2, x1, p["w_up"], p["ffn_conv_w"], p["ffn_conv_b"], p["w_down"], norm_f, seq_len,
                 tl["tm_ffn"], tl["tf_ffn"])
        return y.reshape(batch, seq_len, d_model)

    return (trunk(x_prompt), trunk(x_sample))
```

```python
import functools
import math

import numpy as np
import jax
import jax.numpy as jnp
from jax import lax
from jax.experimental import pallas as pl
from jax.experimental.pallas import tpu as pltpu

ATTN_HEADS = 8
ATTN_QK_DIM = 64
ATTN_V_DIM = 2 * ATTN_QK_DIM
ATTN_QK_WIDTH = ATTN_HEADS * 2 * ATTN_QK_DIM
ATTN_WIDTH = ATTN_HEADS * ATTN_V_DIM
HYENA_EMB_DIM = 33
HYENA_BANDS = (HYENA_EMB_DIM - 1) // 2
DECAY_TARGET = 1e-2
FAST_DECAY_PCT = 0.3
SLOW_DECAY_PCT = 1.5
ROPE_THETA = 10000.0
NORM_EPS = 1e-6

LANES = 128
BF16_SUBLANES = 16
VMEM_LIMIT_BYTES = 56 * 1024 * 1024

DFT_INNER = LANES

F32 = jnp.float32
BF16 = jnp.bfloat16


def _cparams(*sem):
    return pltpu.CompilerParams(dimension_semantics=sem, vmem_limit_bytes=VMEM_LIMIT_BYTES)


def _pick(n, pref):
    t = min(n, pref)
    while n % t:
        t //= 2
    return t


def _rmsnorm_kernel(x_ref, g_ref, o_ref):
    x = x_ref[...]
    ms = jnp.mean(x * x, axis=-1, keepdims=True)
    o_ref[...] = (x * lax.rsqrt(ms + NORM_EPS) * g_ref[...]).astype(o_ref.dtype)


def _rmsnorm(x, g, tm):
    t, d = x.shape
    return pl.pallas_call(
        _rmsnorm_kernel,
        grid=(t // tm,),
        in_specs=[pl.BlockSpec((tm, d), lambda i: (i, 0)),
                  pl.BlockSpec((1, d), lambda i: (0, 0))],
        out_specs=pl.BlockSpec((tm, d), lambda i: (i, 0)),
        out_shape=jax.ShapeDtypeStruct((t, d), BF16),
        compiler_params=_cparams("parallel"),
        name="rmsnorm_in",
    )(x, g.reshape(1, d))


def _proj_plain_kernel(h_ref, w_ref, o_ref):
    o_ref[...] = jnp.dot(h_ref[...], w_ref[...], preferred_element_type=F32).astype(o_ref.dtype)


def _proj_plain(h, w, tm, tn):
    t, d = h.shape
    n = w.shape[1]
    return pl.pallas_call(
        _proj_plain_kernel,
        grid=(t // tm, n // tn),
        in_specs=[pl.BlockSpec((tm, d), lambda i, j: (i, 0)),
                  pl.BlockSpec((d, tn), lambda i, j: (0, j))],
        out_specs=pl.BlockSpec((tm, tn), lambda i, j: (i, j)),
        out_shape=jax.ShapeDtypeStruct((t, n), BF16),
        compiler_params=_cparams("parallel", "arbitrary"),
        name="proj_plain",
    )(h, w)


def _proj_transposed_kernel(wt_ref, h_ref, o_ref):
    nt = (((1,), (1,)), ((), ()))
    o_ref[...] = lax.dot_general(wt_ref[...], h_ref[...], nt, preferred_element_type=F32).astype(o_ref.dtype)


def _proj_transposed(h, wt, batch, seq_len, tm):
    d = h.shape[1]
    n = wt.shape[0]
    nblk = seq_len // tm
    return pl.pallas_call(
        _proj_transposed_kernel,
        grid=(batch, nblk),
        in_specs=[pl.BlockSpec((n, d), lambda b, i: (0, 0)),
                  pl.BlockSpec((tm, d), lambda b, i: (b * nblk + i, 0))],
        out_specs=pl.BlockSpec((None, n, tm), lambda b, i: (b, 0, i)),
        out_shape=jax.ShapeDtypeStruct((batch, n, seq_len), BF16),
        compiler_params=_cparams("parallel", "parallel"),
        name="proj_transposed",
    )(wt, h)


def _proj_gate_kernel(h_ref, w_ref, b_ref, o_ref):
    z = jnp.dot(h_ref[...], w_ref[...], preferred_element_type=F32) + b_ref[...]
    o_ref[...] = (1.0 / (1.0 + jnp.exp(-z))).astype(o_ref.dtype)


def _proj_gate(h, w, b, tm, tn):
    t, d = h.shape
    n = w.shape[1]
    return pl.pallas_call(
        _proj_gate_kernel,
        grid=(t // tm, n // tn),
        in_specs=[pl.BlockSpec((tm, d), lambda i, j: (i, 0)),
                  pl.BlockSpec((d, tn), lambda i, j: (0, j)),
                  pl.BlockSpec((1, tn), lambda i, j: (0, j))],
        out_specs=pl.BlockSpec((tm, tn), lambda i, j: (i, j)),
        out_shape=jax.ShapeDtypeStruct((t, n), BF16),
        compiler_params=_cparams("parallel", "arbitrary"),
        name="proj_gate",
    )(h, w, b.reshape(1, n))


def _proj_rope_kernel(h_ref, w_ref, cos_ref, sin_ref, o_ref, *, q_scale):
    acc = jnp.dot(h_ref[...], w_ref[...], preferred_element_type=F32)
    cos = cos_ref[...]
    sin = sin_ref[...]
    lane = lax.broadcasted_iota(jnp.int32, cos.shape, 1)
    first_half = (lane % ATTN_QK_DIM) < (ATTN_QK_DIM // 2)
    scale = jnp.where(pl.program_id(1) == 0, q_scale, 1.0).astype(F32)
    half = ATTN_QK_DIM // 2
    for g in range(acc.shape[1] // LANES):
        x = acc[:, g * LANES:(g + 1) * LANES]
        up = pltpu.roll(x, LANES - half, 1)
        down = pltpu.roll(x, half, 1)
        rot = jnp.where(first_half, up, down)
        o_ref[:, g * LANES:(g + 1) * LANES] = ((x * cos + rot * sin) * scale).astype(o_ref.dtype)


def _proj_rope(h, w, cos, sin, seq_len, tm):
    t, d = h.shape
    n = w.shape[1]
    tn = ATTN_QK_WIDTH
    nblk = seq_len // tm
    kern = functools.partial(_proj_rope_kernel, q_scale=ATTN_QK_DIM ** -0.5 * math.log2(math.e))
    return pl.pallas_call(
        kern,
        grid=(t // tm, n // tn),
        in_specs=[pl.BlockSpec((tm, d), lambda i, j: (i, 0)),
                  pl.BlockSpec((d, tn), lambda i, j: (0, j)),
                  pl.BlockSpec((tm, LANES), lambda i, j: (i % nblk, 0)),
                  pl.BlockSpec((tm, LANES), lambda i, j: (i % nblk, 0))],
        out_specs=pl.BlockSpec((tm, tn), lambda i, j: (i, j)),
        out_shape=jax.ShapeDtypeStruct((t, n), BF16),
        compiler_params=_cparams("parallel", "arbitrary"),
        name="proj_rope",
    )(h, w, cos, sin)


def _rope_tables(seq_len):
    d = ATTN_QK_DIM
    inv = ROPE_THETA ** (-jnp.arange(0, d, 2, dtype=F32) / d)
    ang = jnp.arange(seq_len, dtype=F32)[:, None] * inv[None]
    ang = jnp.concatenate([ang, ang], -1)
    sign = jnp.concatenate([-jnp.ones((d // 2,), F32), jnp.ones((d // 2,), F32)])
    cos = jnp.cos(ang)
    sin = jnp.sin(ang) * sign
    reps = LANES // d
    return jnp.tile(cos, (1, reps)), jnp.tile(sin, (1, reps))


def _attn_kernel(q_ref, k_ref, v_ref, lq1_ref, lk1_ref, lq2_ref, lk2_ref, g_ref, o_ref,
                 sa0_ref, sb0_ref, sa1_ref, sb1_ref, m0_ref, m1_ref, acc0_ref, acc1_ref,
                 *, tq, tk, lambda_init):
    seq = k_ref.shape[0]
    n_chunks = seq // tk
    n_tiles = seq // tq
    lane = lax.broadcasted_iota(jnp.int32, (tq, LANES), 1)
    ones_rows = (lax.broadcasted_iota(jnp.int32, (BF16_SUBLANES, tk), 0) == 0).astype(BF16)
    nt = (((1,), (1,)), ((), ()))
    score_refs = ((sa0_ref, sb0_ref), (sa1_ref, sb1_ref))
    lam = (jnp.exp(jnp.sum(lq1_ref[...] * lk1_ref[...], axis=1, keepdims=True))
           - jnp.exp(jnp.sum(lq2_ref[...] * lk2_ref[...], axis=1, keepdims=True))
           + lambda_init)

    def tile_rows(tile):
        start = tile * tq if isinstance(tile, int) else pl.multiple_of(tile * tq, tq)
        return pl.ds(start, tq)

    def split_maps(tile):
        q = q_ref[tile_rows(tile), :]
        zero = jnp.zeros_like(q)
        return jnp.where(lane < ATTN_QK_DIM, q, zero), jnp.where(lane >= ATTN_QK_DIM, q, zero)

    def scores(qmaps, j, slot):
        kc = k_ref[j * tk:(j + 1) * tk, :]
        sa_ref, sb_ref = score_refs[slot]
        sa_ref[...] = lax.dot_general(kc, qmaps[0], nt, preferred_element_type=F32)
        sb_ref[...] = lax.dot_general(kc, qmaps[1], nt, preferred_element_type=F32)

    def one_map(s_ref, vext, m_ref, acc_ref, first):
        s = s_ref[...]
        m_cur = jnp.max(s, axis=0, keepdims=True)
        if first:
            m_new = m_cur
        else:
            m_old = m_ref[...]
            m_new = jnp.maximum(m_old, m_cur)
        p = jnp.exp2(s - m_new).astype(BF16)
        pv = jnp.dot(vext, p, preferred_element_type=F32)
        if first:
            acc_ref[...] = pv
        else:
            acc_ref[...] = jnp.exp2(m_old - m_new) * acc_ref[...] + pv
        m_ref[...] = m_new

    def weighted_values(j, slot):
        vext = jnp.concatenate([v_ref[:, j * tk:(j + 1) * tk], ones_rows], axis=0)
        sa_ref, sb_ref = score_refs[slot]
        one_map(sa_ref, vext, m0_ref, acc0_ref, j == 0)
        one_map(sb_ref, vext, m1_ref, acc1_ref, j == 0)

    def finish(tile):
        a0 = acc0_ref[...]
        a1 = acc1_ref[...]
        o0 = a0[:LANES, :] / a0[LANES:LANES + 1, :]
        o1 = a1[:LANES, :] / a1[LANES:LANES + 1, :]
        o = o0 - lam * o1
        ms = jnp.mean(o * o, axis=0, keepdims=True)
        gain = jnp.concatenate([g_ref[...]] * (tq // LANES), axis=1)
        o = o * lax.rsqrt(ms + NORM_EPS) * gain
        o_ref[tile_rows(tile), :] = (o * (1.0 - lambda_init)).T.astype(o_ref.dtype)

    scores(split_maps(0), 0, 0)
    acc0_ref[...] = jnp.ones(acc0_ref.shape, F32)
    acc1_ref[...] = jnp.ones(acc1_ref.shape, F32)

    def tile_body(tile, carry):
        qmaps = split_maps(tile)
        for j in range(0, n_chunks, 2):
            scores(qmaps, j + 1, 1)
            if j == 0:
                finish(jnp.maximum(tile - 1, 0))
            weighted_values(j, 0)
            if j + 2 < n_chunks:
                scores(qmaps, j + 2, 0)
            else:
                scores(split_maps(jnp.minimum(tile + 1, n_tiles - 1)), 0, 0)
            weighted_values(j + 1, 1)
        return carry

    lax.fori_loop(0, n_tiles, tile_body, 0)
    finish(n_tiles - 1)


def _attention(qk, vt, lq1, lk1, lq2, lk2, subln_g, batch, seq_len, lambda_init, tq, tk):
    assert (seq_len // tk) % 2 == 0, "key chunks are processed in pairs"
    kern = functools.partial(_attn_kernel, tq=tq, tk=tk, lambda_init=lambda_init)
    small = pl.BlockSpec((1, ATTN_QK_DIM), lambda b, h: (0, 0))
    acc_rows = ATTN_V_DIM + BF16_SUBLANES
    gain = jnp.tile(subln_g.reshape(-1, 1), (1, LANES))
    return pl.pallas_call(
        kern,
        grid=(batch, ATTN_HEADS),
        in_specs=[pl.BlockSpec((None, seq_len, LANES), lambda b, h: (b, 0, h)),
                  pl.BlockSpec((None, seq_len, LANES), lambda b, h: (b, 0, ATTN_HEADS + h)),
                  pl.BlockSpec((None, ATTN_V_DIM, seq_len), lambda b, h: (b, h, 0)),
                  small, small, small, small,
                  pl.BlockSpec((ATTN_V_DIM, LANES), lambda b, h: (0, 0))],
        out_specs=pl.BlockSpec((None, seq_len, LANES), lambda b, h: (b, 0, h)),
        out_shape=jax.ShapeDtypeStruct((batch, seq_len, ATTN_WIDTH), BF16),
        scratch_shapes=[pltpu.VMEM((tk, tq), F32)] * 4
                       + [pltpu.VMEM((1, tq), F32), pltpu.VMEM((1, tq), F32),
                          pltpu.VMEM((acc_rows, tq), F32), pltpu.VMEM((acc_rows, tq), F32)],
        compiler_params=_cparams("parallel", "parallel"),
        name="diff_attention",
    )(qk, qk, vt, lq1.reshape(1, -1), lk1.reshape(1, -1), lq2.reshape(1, -1), lk2.reshape(1, -1), gain)


def _conv3_rows(e, w, n_rows, halo):
    tot = e.shape[0]
    prev = pltpu.roll(e, 1, 0)
    nxt = pltpu.roll(e, tot - 1, 0)
    c = prev * w[0:1, :] + e * w[1:2, :] + nxt * w[2:3, :]
    return c[halo:halo + n_rows, :]


def _hyena_prep_kernel(prev_ref, main_ref, next_ref, w_ref, b_ref, u_ref, x0_ref, *, blocks_per_seq):
    i = pl.program_id(0)
    first = (i % blocks_per_seq) == 0
    last = (i % blocks_per_seq) == blocks_per_seq - 1
    halo = prev_ref.shape[0]
    tm = main_ref.shape[0]
    prev = jnp.where(first, 0.0, prev_ref[...].astype(F32))
    nxt = jnp.where(last, 0.0, next_ref[...].astype(F32))
    e = jnp.concatenate([prev, main_ref[...].astype(F32), nxt], axis=0)
    c = _conv3_rows(e, w_ref[...], tm, halo) + b_ref[...]
    width = u_ref.shape[1]
    x0 = c[:, :width]
    x1 = c[:, width:2 * width]
    hv = c[:, 2 * width:]
    u_ref[...] = (hv * x1).astype(u_ref.dtype)
    x0_ref[...] = x0.astype(x0_ref.dtype)


def _hyena_prep(hy, conv_w, conv_b, seq_len, tm):
    t, n = hy.shape
    width = n // 3
    halo = BF16_SUBLANES
    r = tm // halo
    nb = t // halo
    kern = functools.partial(_hyena_prep_kernel, blocks_per_seq=seq_len // tm)
    return pl.pallas_call(
        kern,
        grid=(t // tm,),
        in_specs=[pl.BlockSpec((halo, n), lambda i: (jnp.maximum(i * r - 1, 0), 0)),
                  pl.BlockSpec((tm, n), lambda i: (i, 0)),
                  pl.BlockSpec((halo, n), lambda i: (jnp.minimum((i + 1) * r, nb - 1), 0)),
                  pl.BlockSpec((3, n), lambda i: (0, 0)),
                  pl.BlockSpec((1, n), lambda i: (0, 0))],
        out_specs=[pl.BlockSpec((tm, width), lambda i: (i, 0)),
                   pl.BlockSpec((tm, width), lambda i: (i, 0))],
        out_shape=[jax.ShapeDtypeStruct((t, width), BF16),
                   jax.ShapeDtypeStruct((t, width), BF16)],
        compiler_params=_cparams("parallel"),
        name="hyena_prep",
    )(hy, hy, hy, conv_w, conv_b.reshape(1, n))


def _filter_kernel(z_ref, w1_ref, b1_ref, w2_ref, b2_ref, w3_ref, b3_ref, w4_ref, f_ref, d_ref, o_ref,
                   *, seq_len):
    hp = lax.Precision.HIGHEST
    z = z_ref[...]
    f = f_ref[...]
    h = jnp.sin(f * (jnp.dot(z, w1_ref[...], precision=hp, preferred_element_type=F32) + b1_ref[...]))
    h = jnp.sin(f * (jnp.dot(h, w2_ref[...], precision=hp, preferred_element_type=F32) + b2_ref[...]))
    h = jnp.sin(f * (jnp.dot(h, w3_ref[...], precision=hp, preferred_element_type=F32) + b3_ref[...]))
    h = jnp.dot(h, w4_ref[...], precision=hp, preferred_element_type=F32)
    t = z[:, 0:1]
    h = h * jnp.exp(-t * d_ref[...])
    row = pl.program_id(0) * z.shape[0] + lax.broadcasted_iota(jnp.int32, (z.shape[0], 1), 0)
    o_ref[...] = jnp.where(row == seq_len, 0.0, h).astype(o_ref.dtype)


def _hyena_filters(seq_len, w1, b1, w2, b2, w3, b3, w4, freq, tl):
    fw = w1.shape[1]
    c = w4.shape[1] // 2
    t = jnp.linspace(0.0, 1.0, seq_len, dtype=F32)[:, None]
    w = 2.0 * math.pi * jnp.arange(seq_len, dtype=F32)[:, None] / seq_len
    f = jnp.linspace(1e-4, HYENA_BANDS - 1, HYENA_BANDS, dtype=F32)[None]
    z = jnp.concatenate([t, jnp.cos(f * w), -jnp.sin(f * w)], -1)
    emb = z.shape[1]
    z = jnp.concatenate([z, jnp.zeros((1, emb), F32), z[1:][::-1]], axis=0)
    z = jnp.pad(z, ((0, 0), (0, LANES - emb)))
    w1p = jnp.pad(w1, ((0, LANES - emb), (0, 0)))
    max_decay = math.log(DECAY_TARGET) / FAST_DECAY_PCT
    min_decay = math.log(DECAY_TARGET) / SLOW_DECAY_PCT
    deltas = jnp.abs(jnp.linspace(min_decay, max_decay, c, dtype=F32)).reshape(1, c)
    full = lambda a: pl.BlockSpec(a.shape, lambda i: (0,) * a.ndim)
    half_blocks = seq_len // tl
    pre = [w1p, b1.reshape(1, fw), w2, b2.reshape(1, fw), w3, b3.reshape(1, fw)]
    post = [freq.reshape(1, fw), deltas]
    kern = functools.partial(_filter_kernel, seq_len=seq_len)
    return pl.pallas_call(
        kern,
        grid=(2 * half_blocks,),
        in_specs=([pl.BlockSpec((tl, LANES), lambda i: (i, 0))] + [full(a) for a in pre]
                  + [pl.BlockSpec((fw, c), lambda i: (0, i // half_blocks))]
                  + [full(a) for a in post]),
        out_specs=pl.BlockSpec((tl, c), lambda i: (i, 0)),
        out_shape=jax.ShapeDtypeStruct((2 * seq_len, c), BF16),
        compiler_params=_cparams("parallel"),
        name="hyena_filters",
    )(z, *pre, w4, *post)


def _dft_outer_kernel(w_ref, x_ref, or_ref, oi_ref, xs_ref, os_ref):
    k, tm2, tc = x_ref.shape
    n1 = or_ref.shape[0]
    groups = tc // LANES
    x = x_ref[...].astype(F32).reshape(k * tm2, tc)
    for g in range(groups):
        xs_ref[g] = x[:, g * LANES:(g + 1) * LANES]
    w = w_ref[...]
    for j in range(tm2):
        xj = jnp.concatenate([xs_ref[g, pl.ds(j, k, stride=tm2), :] for g in range(groups)], axis=1)
        res = jnp.dot(w, xj.astype(BF16), preferred_element_type=F32)
        for g in range(groups):
            os_ref[g, pl.ds(j, 2 * n1, stride=tm2), :] = res[:, g * LANES:(g + 1) * LANES]
    for g in range(groups):
        o = os_ref[g].astype(or_ref.dtype).reshape(2 * n1, tm2, LANES)
        or_ref[:, :, g * LANES:(g + 1) * LANES] = o[:n1]
        oi_ref[:, :, g * LANES:(g + 1) * LANES] = o[n1:]


def _dft_outer(w, x4, tm2, tc):
    g, k, n2, c = x4.shape
    n1 = w.shape[0] // 2
    out = jax.ShapeDtypeStruct((g, n1, n2, c), BF16)
    return pl.pallas_call(
        _dft_outer_kernel,
        grid=(g, n2 // tm2, c // tc),
        in_specs=[pl.BlockSpec(w.shape, lambda a, i, j: (0, 0)),
                  pl.BlockSpec((None, k, tm2, tc), lambda a, i, j: (a, 0, i, j))],
        out_specs=[pl.BlockSpec((None, n1, tm2, tc), lambda a, i, j: (a, 0, i, j)),
                   pl.BlockSpec((None, n1, tm2, tc), lambda a, i, j: (a, 0, i, j))],
        out_shape=[out, out],
        scratch_shapes=[pltpu.VMEM((tc // LANES, k * tm2, LANES), F32),
                        pltpu.VMEM((tc // LANES, 2 * n1 * tm2, LANES), F32)],
        compiler_params=_cparams("parallel", "parallel", "parallel"),
        name="dft_outer",
    )(w, x4)


def _spectrum_kernel(ar_ref, ai_ref, wf_ref, o_ref, s_ref, *, scale):
    n2 = ar_ref.shape[0]
    s_ref[0:n2, :] = ar_ref[...]
    s_ref[n2:2 * n2, :] = ai_ref[...]
    o_ref[...] = jnp.dot(wf_ref[...], s_ref[...], preferred_element_type=F32) * scale


def _filter_spectrum(ar, ai, wf, scale):
    n1, n2, c = ar.shape
    r = 2 * n2
    kern = functools.partial(_spectrum_kernel, scale=scale)
    slab = pl.BlockSpec((None, n2, c), lambda k: (k, 0, 0))
    return pl.pallas_call(
        kern,
        grid=(n1,),
        in_specs=[slab, slab, pl.BlockSpec((None, r, r), lambda k: (k, 0, 0))],
        out_specs=pl.BlockSpec((None, r, c), lambda k: (k, 0, 0)),
        out_shape=jax.ShapeDtypeStruct((n1, r, c), F32),
        scratch_shapes=[pltpu.VMEM((r, c), BF16)],
        compiler_params=_cparams("parallel"),
        name="filter_spectrum",
    )(ar, ai, wf)


def _fft_mid_kernel(ar_ref, ai_ref, h_ref, wf_ref, wi_ref, zr_ref, zi_ref, s_ref):
    n2 = ar_ref.shape[0]
    s_ref[0:n2, :] = ar_ref[...]
    s_ref[n2:2 * n2, :] = ai_ref[...]
    x = jnp.dot(wf_ref[...], s_ref[...], preferred_element_type=F32)
    h = h_ref[...]
    xr, xi = x[:n2], x[n2:]
    hr, hi = h[:n2], h[n2:]
    s_ref[0:n2, :] = (xr * hr - xi * hi).astype(s_ref.dtype)
    s_ref[n2:2 * n2, :] = (xr * hi + xi * hr).astype(s_ref.dtype)
    z = jnp.dot(wi_ref[...], s_ref[...], preferred_element_type=F32)
    zr_ref[...] = z[:n2].astype(zr_ref.dtype)
    zi_ref[...] = z[n2:].astype(zi_ref.dtype)


def _fft_mid(ar, ai, hspec, wf, wi):
    batch, n1, n2, c = ar.shape
    r = 2 * n2
    slab = pl.BlockSpec((None, None, n2, c), lambda k, b: (b, k, 0, 0))
    out = jax.ShapeDtypeStruct((batch, n1, n2, c), BF16)
    return pl.pallas_call(
        _fft_mid_kernel,
        grid=(n1, batch),
        in_specs=[slab, slab,
                  pl.BlockSpec((None, r, c), lambda k, b: (k, 0, 0)),
                  pl.BlockSpec((None, r, r), lambda k, b: (k, 0, 0)),
                  pl.BlockSpec((None, r, r), lambda k, b: (k, 0, 0))],
        out_specs=[slab, slab],
        out_shape=[out, out],
        scratch_shapes=[pltpu.VMEM((r, c), BF16)],
        compiler_params=_cparams("parallel", "arbitrary"),
        name="fft_mid",
    )(ar, ai, hspec, wf, wi)


def _fft_out_kernel(w_ref, zr_ref, zi_ref, u_ref, x0_ref, b_ref, o_ref, rs_ref, is_ref, ys_ref):
    n1, tm2, tc = zr_ref.shape
    n1h = o_ref.shape[0]
    groups = tc // LANES
    zr = zr_ref[...].astype(F32).reshape(n1 * tm2, tc)
    zi = zi_ref[...].astype(F32).reshape(n1 * tm2, tc)
    for g in range(groups):
        rs_ref[g] = zr[:, g * LANES:(g + 1) * LANES]
        is_ref[g] = zi[:, g * LANES:(g + 1) * LANES]
    w = w_ref[...]
    for j in range(tm2):
        parts = [jnp.concatenate([rs_ref[g, pl.ds(j, n1, stride=tm2), :] for g in range(groups)], axis=1),
                 jnp.concatenate([is_ref[g, pl.ds(j, n1, stride=tm2), :] for g in range(groups)], axis=1)]
        if w.shape[1] > 2 * n1:
            parts.append(jnp.zeros((w.shape[1] - 2 * n1, tc), F32))
        zj = jnp.concatenate(parts, axis=0)
        y = jnp.dot(w, zj.astype(BF16), preferred_element_type=F32)
        for g in range(groups):
            ys_ref[g, pl.ds(j, n1h, stride=tm2), :] = y[:, g * LANES:(g + 1) * LANES]
    for g in range(groups):
        sl = slice(g * LANES, (g + 1) * LANES)
        y = ys_ref[g].reshape(n1h, tm2, LANES)
        u = u_ref[:, :, sl].astype(F32)
        o_ref[:, :, sl] = ((y + u * b_ref[:, sl]) * x0_ref[:, :, sl].astype(F32)).astype(o_ref.dtype)


def _fft_out(w, zr, zi, u4, x04, bias, tm2, tc):
    batch, n1, n2, c = zr.shape
    n1h = w.shape[0]
    zspec = pl.BlockSpec((None, n1, tm2, tc), lambda a, i, j: (a, 0, i, j))
    uspec = pl.BlockSpec((None, n1h, tm2, tc), lambda a, i, j: (a, 0, i, j))
    return pl.pallas_call(
        _fft_out_kernel,
        grid=(batch, n2 // tm2, c // tc),
        in_specs=[pl.BlockSpec(w.shape, lambda a, i, j: (0, 0)), zspec, zspec, uspec, uspec,
                  pl.BlockSpec((1, tc), lambda a, i, j: (0, j))],
        out_specs=uspec,
        out_shape=jax.ShapeDtypeStruct((batch, n1h, n2, c), BF16),
        scratch_shapes=[pltpu.VMEM((tc // LANES, n1 * tm2, LANES), F32),
                        pltpu.VMEM((tc // LANES, n1 * tm2, LANES), F32),
                        pltpu.VMEM((tc // LANES, n1h * tm2, LANES), F32)],
        compiler_params=_cparams("parallel", "parallel", "parallel"),
        name="fft_out",
    )(w, zr, zi, u4, x04, bias.reshape(1, c))


def _dft_tables(seq_len):
    n = 2 * seq_len
    n2 = DFT_INNER
    n1 = n // n2
    n1h = n1 // 2
    nk = n1h + 8
    k1 = np.arange(nk)[:, None]
    m1 = np.arange(n1)[None, :]
    ang1 = 2.0 * np.pi * k1 * m1 / n1
    w1_full = np.concatenate([np.cos(ang1), -np.sin(ang1)], axis=0)
    w1_half = w1_full[:, :n1h]
    weight = np.where((np.arange(nk) == 0) | (np.arange(nk) == n1h), 1.0, 2.0) * (np.arange(nk) <= n1h)
    ang1i = ang1.T[:n1h]
    w1_inv = np.concatenate([np.cos(ang1i) * weight, -np.sin(ang1i) * weight], axis=1)
    w1_inv = np.pad(w1_inv, ((0, 0), (0, -(2 * nk) % LANES)))
    k1d = jnp.arange(nk, dtype=jnp.int32)[:, None, None]
    k2d = jnp.arange(n2, dtype=jnp.int32)[None, :, None]
    m2d = jnp.arange(n2, dtype=jnp.int32)[None, None, :]
    phase = (m2d * (k1d + n1 * k2d)) % n
    ang = phase.astype(F32) * (2.0 * math.pi / n)
    c, s = jnp.cos(ang), jnp.sin(ang)
    w2_fwd = jnp.concatenate([jnp.concatenate([c, s], axis=2), jnp.concatenate([-s, c], axis=2)], axis=1)
    ct, st = jnp.swapaxes(c, 1, 2), jnp.swapaxes(s, 1, 2)
    w2_inv = jnp.concatenate([jnp.concatenate([ct, -st], axis=2), jnp.concatenate([st, ct], axis=2)], axis=1)
    bf = lambda a: jnp.asarray(a, F32).astype(BF16)
    return dict(n1=n1, n2=n2, n1h=n1h, w1_full=bf(w1_full), w1_half=bf(w1_half), w1_inv=bf(w1_inv),
                w2_fwd=w2_fwd.astype(BF16), w2_inv=w2_inv.astype(BF16))


def _hyena_long_conv(u, x0, circ, bias, batch, seq_len, tab, tm2, tc):
    c = u.shape[1]
    n1, n2, n1h = tab["n1"], tab["n2"], tab["n1h"]
    tc = _pick(c, tc)
    far, fai = _dft_outer(tab["w1_full"], circ.reshape(1, n1, n2, c), tm2, tc)
    hspec = _filter_spectrum(far[0], fai[0], tab["w2_fwd"], 1.0 / (n1 * n2))
    u4 = u.reshape(batch, n1h, n2, c)
    x04 = x0.reshape(batch, n1h, n2, c)
    ar, ai = _dft_outer(tab["w1_half"], u4, tm2, tc)
    zr, zi = _fft_mid(ar, ai, hspec, tab["w2_fwd"], tab["w2_inv"])
    y = _fft_out(tab["w1_inv"], zr, zi, u4, x04, bias, tm2, tc)
    return y.reshape(batch * seq_len, c)


def _merge_kernel(attn_ref, hy_ref, gate_ref, x_ref, wa_ref, wh_ref, wo_ref, g2_ref,
                  x1_ref, hn_ref):
    d = x_ref.shape[1]
    pa = jnp.dot(attn_ref[...], wa_ref[...], preferred_element_type=F32)
    ph = jnp.dot(hy_ref[...], wh_ref[...], preferred_element_type=F32)
    merged = gate_ref[:, :d].astype(F32) * pa + gate_ref[:, d:].astype(F32) * ph
    x1 = x_ref[...] + jnp.dot(merged.astype(BF16), wo_ref[...], preferred_element_type=F32)
    x1_ref[...] = x1
    ms = jnp.mean(x1 * x1, axis=-1, keepdims=True)
    hn_ref[...] = (x1 * lax.rsqrt(ms + NORM_EPS) * g2_ref[...]).astype(hn_ref.dtype)


def _merge(attn, hyena, gates, x, wa, wh, wo, norm2, tm):
    t, d = x.shape
    const = lambda a: pl.BlockSpec(a.shape, lambda i: (0, 0), pipeline_mode=pl.Buffered(1))
    row = lambda n: pl.BlockSpec((tm, n), lambda i: (i, 0))
    g2 = norm2.reshape(1, d)
    return pl.pallas_call(
        _merge_kernel,
        grid=(t // tm,),
        in_specs=[row(attn.shape[1]), row(hyena.shape[1]), row(gates.shape[1]), row(d),
                  const(wa), const(wh), const(wo), const(g2)],
        out_specs=[row(d), row(d)],
        out_shape=[jax.ShapeDtypeStruct((t, d), F32), jax.ShapeDtypeStruct((t, d), BF16)],
        compiler_params=_cparams("parallel"),
        name="merge_out_proj",
    )(attn, hyena, gates, x, wa, wh, wo, g2)


def _ffn_kernel(prev_ref, main_ref, next_ref, x1_ref, wg_ref, wv_ref, cg_ref, cv_ref, bg_ref, bv_ref,
                wd_ref, gf_ref, o_ref, ext_ref, acc_ref, *, blocks_per_seq):
    i = pl.program_id(0)
    j = pl.program_id(1)
    halo = prev_ref.shape[0]
    tm = main_ref.shape[0]

    @pl.when(j == 0)
    def _():
        first = (i % blocks_per_seq) == 0
        last = (i % blocks_per_seq) == blocks_per_seq - 1
        zero = jnp.zeros(prev_ref.shape, prev_ref.dtype)
        ext_ref[0:halo, :] = jnp.where(first, zero, prev_ref[...])
        ext_ref[halo:halo + tm, :] = main_ref[...]
        ext_ref[halo + tm:, :] = jnp.where(last, zero, next_ref[...])
        acc_ref[...] = jnp.zeros(acc_ref.shape, F32)

    ext = ext_ref[...]
    ug = jnp.dot(ext, wg_ref[...], preferred_element_type=F32)
    uv = jnp.dot(ext, wv_ref[...], preferred_element_type=F32)
    ug = _conv3_rows(ug, cg_ref[...], tm, halo) + bg_ref[...]
    uv = _conv3_rows(uv, cv_ref[...], tm, halo) + bv_ref[...]
    act = (ug / (1.0 + jnp.exp(-ug))) * uv
    acc_ref[...] += jnp.dot(act.astype(BF16), wd_ref[...], preferred_element_type=F32)

    @pl.when(j == pl.num_programs(1) - 1)
    def _():
        y = x1_ref[...] + acc_ref[...]
        ms = jnp.mean(y * y, axis=-1, keepdims=True)
        o_ref[...] = y * lax.rsqrt(ms + NORM_EPS) * gf_ref[...]


def _ffn(hn2, x1, w_up, conv_w, conv_b, w_down, norm_f, seq_len, tm, tf):
    t, d = x1.shape
    dff = w_down.shape[0]
    nj = dff // tf
    halo = BF16_SUBLANES
    r = tm // halo
    nb = t // halo
    kern = functools.partial(_ffn_kernel, blocks_per_seq=seq_len // tm)
    cb = conv_b.reshape(1, 2 * dff)
    return pl.pallas_call(
        kern,
        grid=(t // tm, nj),
        in_specs=[pl.BlockSpec((halo, d), lambda i, j: (jnp.maximum(i * r - 1, 0), 0)),
                  pl.BlockSpec((tm, d), lambda i, j: (i, 0)),
                  pl.BlockSpec((halo, d), lambda i, j: (jnp.minimum((i + 1) * r, nb - 1), 0)),
                  pl.BlockSpec((tm, d), lambda i, j: (i, 0)),
                  pl.BlockSpec((d, tf), lambda i, j: (0, j)),
                  pl.BlockSpec((d, tf), lambda i, j: (0, nj + j)),
                  pl.BlockSpec((3, tf), lambda i, j: (0, j)),
                  pl.BlockSpec((3, tf), lambda i, j: (0, nj + j)),
                  pl.BlockSpec((1, tf), lambda i, j: (0, j)),
                  pl.BlockSpec((1, tf), lambda i, j: (0, nj + j)),
                  pl.BlockSpec((tf, d), lambda i, j: (j, 0)),
                  pl.BlockSpec((1, d), lambda i, j: (0, 0))],
        out_specs=pl.BlockSpec((tm, d), lambda i, j: (i, 0)),
        out_shape=jax.ShapeDtypeStruct((t, d), F32),
        scratch_shapes=[pltpu.VMEM((tm + 2 * halo, d), BF16), pltpu.VMEM((tm, d), F32)],
        compiler_params=_cparams("parallel", "arbitrary"),
        name="conv_glu_mlp",
    )(hn2, hn2, hn2, x1, w_up, w_up, conv_w, conv_w, cb, cb, w_down, norm_f.reshape(1, d))


def _tiles(seq_len, d_model, d_ff):
    return dict(
        tm_norm=_pick(seq_len, 512),
        tm_proj=_pick(seq_len, 1024),
        tn_proj=1024,
        tq=_pick(seq_len, 512),
        tk=_pick(seq_len, 1024),
        tm_prep=_pick(seq_len, 512),
        tl_filt=_pick(seq_len, 512),
        tm2_dft=BF16_SUBLANES,
        tc_dft=256,
        tm_merge=_pick(seq_len, 256),
        tm_ffn=_pick(seq_len, 512),
        tf_ffn=_pick(d_ff, 512),
    )


def _encoder_layer(x, batch, seq_len, lambda_init, p, consts):
    t, d = x.shape
    tl = _tiles(seq_len, d, p["w_down"].shape[0])
    hn = _rmsnorm(x, p["norm1"], tl["tm_norm"])

    qk = _proj_rope(hn, p["w_qk"], consts["cos"], consts["sin"], seq_len, tl["tm_proj"])
    vt = _proj_transposed(hn, p["w_v_t"], batch, seq_len, tl["tm_proj"])
    hy = _proj_plain(hn, p["w_hy"], tl["tm_proj"], _pick(p["w_hy"].shape[1], tl["tn_proj"]))
    gates = _proj_gate(hn, p["w_gl"], p["gate_b"], tl["tm_proj"], _pick(p["w_gl"].shape[1], tl["tn_proj"]))

    attn = _attention(qk.reshape(batch, seq_len, -1), vt,
                      p["lambda_q1"], p["lambda_k1"], p["lambda_q2"], p["lambda_k2"], p["subln_g"],
                      batch, seq_len, lambda_init, tl["tq"], tl["tk"]).reshape(t, ATTN_WIDTH)

    u, x0 = _hyena_prep(hy, p["in_conv_w"], p["in_conv_b"], seq_len, tl["tm_prep"])
    circ = _hyena_filters(seq_len, p["filt_w1"], p["filt_b1"], p["filt_w2"], p["filt_b2"],
                          p["filt_w3"], p["filt_b3"], p["filt_w4"], p["filt_freq"], tl["tl_filt"])
    hyena = _hyena_long_conv(u, x0, circ, p["hyena_bias"], batch, seq_len, consts["dft"],
                             tl["tm2_dft"], tl["tc_dft"])

    x1, hn2 = _merge(attn, hyena, gates, x, p["w_attn_out"], p["w_hyena_out"], p["w_out"], p["norm2"],
                     tl["tm_merge"])
    return x1, hn2, tl


def kernel(x_prompt, x_sample, norm1, w_in, in_conv_w, in_conv_b, gate_b, lambda_q1, lambda_k1, lambda_q2, lambda_k2, subln_g, filt_w1, filt_b1, filt_w2, filt_b2, filt_w3, filt_b3, filt_w4, filt_freq, hyena_bias, w_attn_out, w_hyena_out, w_out, norm2, w_up, ffn_conv_w, ffn_conv_b, w_down, norm_f):
    depth = norm1.shape[0]
    assert depth == 1, "this kernel implements the single-layer trunk"
    d_model = x_prompt.shape[-1]
    hyena_width = hyena_bias.shape[-1]
    o1 = ATTN_QK_WIDTH
    o2 = o1 + ATTN_QK_WIDTH
    o3 = o2 + ATTN_WIDTH
    o4 = o3 + 3 * hyena_width

    layers = []
    for i in range(depth):
        wi = w_in[i].astype(BF16)
        layers.append(dict(
            norm1=norm1[i], w_qk=wi[:, :o2], w_v_t=wi[:, o2:o3].T, w_hy=wi[:, o3:o4], w_gl=wi[:, o4:],
            in_conv_w=in_conv_w[i], in_conv_b=in_conv_b[i], gate_b=gate_b[i],
            lambda_q1=lambda_q1[i], lambda_k1=lambda_k1[i], lambda_q2=lambda_q2[i], lambda_k2=lambda_k2[i],
            subln_g=subln_g[i],
            filt_w1=filt_w1[i], filt_b1=filt_b1[i], filt_w2=filt_w2[i], filt_b2=filt_b2[i],
            filt_w3=filt_w3[i], filt_b3=filt_b3[i], filt_w4=filt_w4[i], filt_freq=filt_freq[i],
            hyena_bias=hyena_bias[i],
            w_attn_out=w_attn_out[i].astype(BF16), w_hyena_out=w_hyena_out[i].astype(BF16),
            w_out=w_out[i].astype(BF16), norm2=norm2[i],
            w_up=w_up[i].astype(BF16), ffn_conv_w=ffn_conv_w[i], ffn_conv_b=ffn_conv_b[i],
            w_down=w_down[i].astype(BF16)))

    def trunk(x3):
        batch, seq_len, _ = x3.shape
        cos, sin = _rope_tables(seq_len)
        consts = dict(cos=cos, sin=sin, dft=_dft_tables(seq_len))
        x = x3.reshape(batch * seq_len, d_model)
        lambda_init = 0.8 - 0.6 * math.exp(-0.3 * 0)
        p = layers[0]
        x1, hn2, tl = _encoder_layer(x, batch, seq_len, lambda_init, p, consts)
        y = _ffn(hn2, x1, p["w_up"], p["ffn_conv_w"], p["ffn_conv_b"], p["w_down"], norm_f, seq_len,
                 tl["tm_ffn"], tl["tf_ffn"])
        return y.reshape(batch, seq_len, d_model)

    return (trunk(x_prompt), trunk(x_sample))
```

```python
import functools
import math

import numpy as np
import jax
import jax.numpy as jnp
from jax import lax
from jax.experimental import pallas as pl
from jax.experimental.pallas import tpu as pltpu

ATTN_HEADS = 8
ATTN_QK_DIM = 64
ATTN_V_DIM = 2 * ATTN_QK_DIM
ATTN_QK_WIDTH = ATTN_HEADS * 2 * ATTN_QK_DIM
ATTN_WIDTH = ATTN_HEADS * ATTN_V_DIM
HYENA_EMB_DIM = 33
HYENA_BANDS = (HYENA_EMB_DIM - 1) // 2
DECAY_TARGET = 1e-2
FAST_DECAY_PCT = 0.3
SLOW_DECAY_PCT = 1.5
ROPE_THETA = 10000.0
NORM_EPS = 1e-6

LANES = 128
BF16_SUBLANES = 16
VMEM_LIMIT_BYTES = 56 * 1024 * 1024

DFT_INNER = LANES

F32 = jnp.float32
BF16 = jnp.bfloat16


def _cparams(*sem):
    return pltpu.CompilerParams(dimension_semantics=sem, vmem_limit_bytes=VMEM_LIMIT_BYTES)


def _pick(n, pref):
    t = min(n, pref)
    while n % t:
        t //= 2
    return t


def _rmsnorm_kernel(x_ref, g_ref, o_ref):
    x = x_ref[...]
    ms = jnp.mean(x * x, axis=-1, keepdims=True)
    o_ref[...] = (x * lax.rsqrt(ms + NORM_EPS) * g_ref[...]).astype(o_ref.dtype)


def _rmsnorm(x, g, tm):
    t, d = x.shape
    return pl.pallas_call(
        _rmsnorm_kernel,
        grid=(t // tm,),
        in_specs=[pl.BlockSpec((tm, d), lambda i: (i, 0)),
                  pl.BlockSpec((1, d), lambda i: (0, 0))],
        out_specs=pl.BlockSpec((tm, d), lambda i: (i, 0)),
        out_shape=jax.ShapeDtypeStruct((t, d), BF16),
        compiler_params=_cparams("parallel"),
        name="rmsnorm_in",
    )(x, g.reshape(1, d))


def _proj_plain_kernel(h_ref, w_ref, o_ref):
    o_ref[...] = jnp.dot(h_ref[...], w_ref[...], preferred_element_type=F32).astype(o_ref.dtype)


def _proj_plain(h, w, tm, tn):
    t, d = h.shape
    n = w.shape[1]
    return pl.pallas_call(
        _proj_plain_kernel,
        grid=(t // tm, n // tn),
        in_specs=[pl.BlockSpec((tm, d), lambda i, j: (i, 0)),
                  pl.BlockSpec((d, tn), lambda i, j: (0, j))],
        out_specs=pl.BlockSpec((tm, tn), lambda i, j: (i, j)),
        out_shape=jax.ShapeDtypeStruct((t, n), BF16),
        compiler_params=_cparams("parallel", "arbitrary"),
        name="proj_plain",
    )(h, w)


def _proj_gate_kernel(h_ref, w_ref, b_ref, o_ref):
    z = jnp.dot(h_ref[...], w_ref[...], preferred_element_type=F32) + b_ref[...]
    o_ref[...] = (1.0 / (1.0 + jnp.exp(-z))).astype(o_ref.dtype)


def _proj_gate(h, w, b, tm, tn):
    t, d = h.shape
    n = w.shape[1]
    return pl.pallas_call(
        _proj_gate_kernel,
        grid=(t // tm, n // tn),
        in_specs=[pl.BlockSpec((tm, d), lambda i, j: (i, 0)),
                  pl.BlockSpec((d, tn), lambda i, j: (0, j)),
                  pl.BlockSpec((1, tn), lambda i, j: (0, j))],
        out_specs=pl.BlockSpec((tm, tn), lambda i, j: (i, j)),
        out_shape=jax.ShapeDtypeStruct((t, n), BF16),
        compiler_params=_cparams("parallel", "arbitrary"),
        name="proj_gate",
    )(h, w, b.reshape(1, n))


def _proj_rope_kernel(h_ref, w_ref, cos_ref, sin_ref, o_ref, *, q_scale):
    acc = jnp.dot(h_ref[...], w_ref[...], preferred_element_type=F32)
    cos = cos_ref[...]
    sin = sin_ref[...]
    lane = lax.broadcasted_iota(jnp.int32, cos.shape, 1)
    first_half = (lane % ATTN_QK_DIM) < (ATTN_QK_DIM // 2)
    scale = jnp.where(pl.program_id(1) == 0, q_scale, 1.0).astype(F32)
    half = ATTN_QK_DIM // 2
    for g in range(acc.shape[1] // LANES):
        x = acc[:, g * LANES:(g + 1) * LANES]
        up = pltpu.roll(x, LANES - half, 1)
        down = pltpu.roll(x, half, 1)
        rot = jnp.where(first_half, up, down)
        o_ref[:, g * LANES:(g + 1) * LANES] = ((x * cos + rot * sin) * scale).astype(o_ref.dtype)


def _proj_rope(h, w, cos, sin, seq_len, tm):
    t, d = h.shape
    n = w.shape[1]
    tn = ATTN_QK_WIDTH
    nblk = seq_len // tm
    kern = functools.partial(_proj_rope_kernel, q_scale=ATTN_QK_DIM ** -0.5 * math.log2(math.e))
    return pl.pallas_call(
        kern,
        grid=(t // tm, n // tn),
        in_specs=[pl.BlockSpec((tm, d), lambda i, j: (i, 0)),
                  pl.BlockSpec((d, tn), lambda i, j: (0, j)),
                  pl.BlockSpec((tm, LANES), lambda i, j: (i % nblk, 0)),
                  pl.BlockSpec((tm, LANES), lambda i, j: (i % nblk, 0))],
        out_specs=pl.BlockSpec((tm, tn), lambda i, j: (i, j)),
        out_shape=jax.ShapeDtypeStruct((t, n), BF16),
        compiler_params=_cparams("parallel", "arbitrary"),
        name="proj_rope",
    )(h, w, cos, sin)


def _rope_tables(seq_len):
    d = ATTN_QK_DIM
    inv = ROPE_THETA ** (-jnp.arange(0, d, 2, dtype=F32) / d)
    ang = jnp.arange(seq_len, dtype=F32)[:, None] * inv[None]
    ang = jnp.concatenate([ang, ang], -1)
    sign = jnp.concatenate([-jnp.ones((d // 2,), F32), jnp.ones((d // 2,), F32)])
    cos = jnp.cos(ang)
    sin = jnp.sin(ang) * sign
    reps = LANES // d
    return jnp.tile(cos, (1, reps)), jnp.tile(sin, (1, reps))


def _attn_kernel(q_ref, k_ref, v_ref, lq1_ref, lk1_ref, lq2_ref, lk2_ref, g_ref, o_ref,
                 sa0_ref, sb0_ref, sa1_ref, sb1_ref, m0_ref, m1_ref, acc0_ref, acc1_ref,
                 *, tq, tk, lambda_init):
    seq = k_ref.shape[0]
    n_chunks = seq // tk
    n_tiles = seq // tq
    lane = lax.broadcasted_iota(jnp.int32, (tq, LANES), 1)
    ones_col = (lax.broadcasted_iota(jnp.int32, (tk, LANES), 1) == 0).astype(BF16)
    nt = (((1,), (1,)), ((), ()))
    score_refs = ((sa0_ref, sb0_ref), (sa1_ref, sb1_ref))
    lam = (jnp.exp(jnp.sum(lq1_ref[...] * lk1_ref[...], axis=1, keepdims=True))
           - jnp.exp(jnp.sum(lq2_ref[...] * lk2_ref[...], axis=1, keepdims=True))
           + lambda_init)

    def tile_rows(tile):
        start = tile * tq if isinstance(tile, int) else pl.multiple_of(tile * tq, tq)
        return pl.ds(start, tq)

    def split_maps(tile):
        q = q_ref[tile_rows(tile), :]
        zero = jnp.zeros_like(q)
        return jnp.where(lane < ATTN_QK_DIM, q, zero), jnp.where(lane >= ATTN_QK_DIM, q, zero)

    def scores(qmaps, j, slot):
        kc = k_ref[j * tk:(j + 1) * tk, :]
        sa_ref, sb_ref = score_refs[slot]
        sa_ref[...] = lax.dot_general(qmaps[0], kc, nt, preferred_element_type=F32)
        sb_ref[...] = lax.dot_general(qmaps[1], kc, nt, preferred_element_type=F32)

    def one_map(s_ref, vext, m_ref, acc_ref, first):
        s = s_ref[...]
        m_cur = jnp.max(s, axis=1, keepdims=True)
        if first:
            m_new = jnp.broadcast_to(m_cur, m_ref.shape)
        else:
            m_old = m_ref[...]
            m_new = jnp.maximum(m_old, m_cur)
        p = jnp.concatenate(
            [jnp.exp2(s[:, c * LANES:(c + 1) * LANES] - m_new) for c in range(tk // LANES)],
            axis=1).astype(BF16)
        pv = jnp.dot(p, vext, preferred_element_type=F32)
        if first:
            acc_ref[...] = pv
        else:
            alpha = jnp.exp2(m_old - m_new)
            acc_ref[...] = jnp.concatenate([alpha, alpha], axis=1) * acc_ref[...] + pv
        m_ref[...] = m_new

    def weighted_values(j, slot):
        vext = jnp.concatenate([v_ref[j * tk:(j + 1) * tk, :], ones_col], axis=1)
        sa_ref, sb_ref = score_refs[slot]
        one_map(sa_ref, vext, m0_ref, acc0_ref, j == 0)
        one_map(sb_ref, vext, m1_ref, acc1_ref, j == 0)

    def finish(tile):
        a0 = acc0_ref[...]
        a1 = acc1_ref[...]
        o0 = a0[:, :LANES] / a0[:, LANES:LANES + 1]
        o1 = a1[:, :LANES] / a1[:, LANES:LANES + 1]
        o = o0 - lam * o1
        ms = jnp.mean(o * o, axis=-1, keepdims=True)
        o = o * lax.rsqrt(ms + NORM_EPS) * g_ref[...]
        o_ref[tile_rows(tile), :] = (o * (1.0 - lambda_init)).astype(o_ref.dtype)

    scores(split_maps(0), 0, 0)
    acc0_ref[...] = jnp.ones(acc0_ref.shape, F32)
    acc1_ref[...] = jnp.ones(acc1_ref.shape, F32)

    def tile_body(tile, carry):
        qmaps = split_maps(tile)
        for j in range(0, n_chunks, 2):
            scores(qmaps, j + 1, 1)
            if j == 0:
                finish(jnp.maximum(tile - 1, 0))
            weighted_values(j, 0)
            if j + 2 < n_chunks:
                scores(qmaps, j + 2, 0)
            else:
                scores(split_maps(jnp.minimum(tile + 1, n_tiles - 1)), 0, 0)
            weighted_values(j + 1, 1)
        return carry

    lax.fori_loop(0, n_tiles, tile_body, 0)
    finish(n_tiles - 1)


def _attention(qk, v, lq1, lk1, lq2, lk2, subln_g, batch, seq_len, lambda_init, tq, tk):
    assert (seq_len // tk) % 2 == 0, "key chunks are processed in pairs"
    kern = functools.partial(_attn_kernel, tq=tq, tk=tk, lambda_init=lambda_init)
    small = pl.BlockSpec((1, ATTN_QK_DIM), lambda b, h: (0, 0))
    return pl.pallas_call(
        kern,
        grid=(batch, ATTN_HEADS),
        in_specs=[pl.BlockSpec((None, seq_len, LANES), lambda b, h: (b, 0, h)),
                  pl.BlockSpec((None, seq_len, LANES), lambda b, h: (b, 0, ATTN_HEADS + h)),
                  pl.BlockSpec((None, seq_len, LANES), lambda b, h: (b, 0, h)),
                  small, small, small, small,
                  pl.BlockSpec((1, LANES), lambda b, h: (0, 0))],
        out_specs=pl.BlockSpec((None, seq_len, LANES), lambda b, h: (b, 0, h)),
        out_shape=jax.ShapeDtypeStruct((batch, seq_len, ATTN_WIDTH), BF16),
        scratch_shapes=[pltpu.VMEM((tq, tk), F32)] * 4
                       + [pltpu.VMEM((tq, LANES), F32), pltpu.VMEM((tq, LANES), F32),
                          pltpu.VMEM((tq, 2 * LANES), F32), pltpu.VMEM((tq, 2 * LANES), F32)],
        compiler_params=_cparams("parallel", "parallel"),
        name="diff_attention",
    )(qk, qk, v, lq1.reshape(1, -1), lk1.reshape(1, -1), lq2.reshape(1, -1), lk2.reshape(1, -1),
      subln_g.reshape(1, -1))


def _conv3_rows(e, w, n_rows, halo):
    tot = e.shape[0]
    prev = pltpu.roll(e, 1, 0)
    nxt = pltpu.roll(e, tot - 1, 0)
    c = prev * w[0:1, :] + e * w[1:2, :] + nxt * w[2:3, :]
    return c[halo:halo + n_rows, :]


def _hyena_prep_kernel(prev_ref, main_ref, next_ref, w_ref, b_ref, u_ref, x0_ref, *, blocks_per_seq):
    i = pl.program_id(0)
    first = (i % blocks_per_seq) == 0
    last = (i % blocks_per_seq) == blocks_per_seq - 1
    halo = prev_ref.shape[0]
    tm = main_ref.shape[0]
    prev = jnp.where(first, 0.0, prev_ref[...].astype(F32))
    nxt = jnp.where(last, 0.0, next_ref[...].astype(F32))
    e = jnp.concatenate([prev, main_ref[...].astype(F32), nxt], axis=0)
    c = _conv3_rows(e, w_ref[...], tm, halo) + b_ref[...]
    width = u_ref.shape[1]
    x0 = c[:, :width]
    x1 = c[:, width:2 * width]
    hv = c[:, 2 * width:]
    u_ref[...] = (hv * x1).astype(u_ref.dtype)
    x0_ref[...] = x0.astype(x0_ref.dtype)


def _hyena_prep(hy, conv_w, conv_b, seq_len, tm):
    t, n = hy.shape
    width = n // 3
    halo = BF16_SUBLANES
    r = tm // halo
    nb = t // halo
    kern = functools.partial(_hyena_prep_kernel, blocks_per_seq=seq_len // tm)
    return pl.pallas_call(
        kern,
        grid=(t // tm,),
        in_specs=[pl.BlockSpec((halo, n), lambda i: (jnp.maximum(i * r - 1, 0), 0)),
                  pl.BlockSpec((tm, n), lambda i: (i, 0)),
                  pl.BlockSpec((halo, n), lambda i: (jnp.minimum((i + 1) * r, nb - 1), 0)),
                  pl.BlockSpec((3, n), lambda i: (0, 0)),
                  pl.BlockSpec((1, n), lambda i: (0, 0))],
        out_specs=[pl.BlockSpec((tm, width), lambda i: (i, 0)),
                   pl.BlockSpec((tm, width), lambda i: (i, 0))],
        out_shape=[jax.ShapeDtypeStruct((t, width), BF16),
                   jax.ShapeDtypeStruct((t, width), BF16)],
        compiler_params=_cparams("parallel"),
        name="hyena_prep",
    )(hy, hy, hy, conv_w, conv_b.reshape(1, n))


def _filter_kernel(z_ref, w1_ref, b1_ref, w2_ref, b2_ref, w3_ref, b3_ref, w4_ref, f_ref, d_ref, o_ref,
                   *, seq_len):
    hp = lax.Precision.HIGHEST
    z = z_ref[...]
    f = f_ref[...]
    h = jnp.sin(f * (jnp.dot(z, w1_ref[...], precision=hp, preferred_element_type=F32) + b1_ref[...]))
    h = jnp.sin(f * (jnp.dot(h, w2_ref[...], precision=hp, preferred_element_type=F32) + b2_ref[...]))
    h = jnp.sin(f * (jnp.dot(h, w3_ref[...], precision=hp, preferred_element_type=F32) + b3_ref[...]))
    h = jnp.dot(h, w4_ref[...], precision=hp, preferred_element_type=F32)
    t = z[:, 0:1]
    h = h * jnp.exp(-t * d_ref[...])
    row = pl.program_id(0) * z.shape[0] + lax.broadcasted_iota(jnp.int32, (z.shape[0], 1), 0)
    o_ref[...] = jnp.where(row == seq_len, 0.0, h).astype(o_ref.dtype)


def _hyena_filters(seq_len, w1, b1, w2, b2, w3, b3, w4, freq, tl):
    fw = w1.shape[1]
    c = w4.shape[1] // 2
    t = jnp.linspace(0.0, 1.0, seq_len, dtype=F32)[:, None]
    w = 2.0 * math.pi * jnp.arange(seq_len, dtype=F32)[:, None] / seq_len
    f = jnp.linspace(1e-4, HYENA_BANDS - 1, HYENA_BANDS, dtype=F32)[None]
    z = jnp.concatenate([t, jnp.cos(f * w), -jnp.sin(f * w)], -1)
    emb = z.shape[1]
    z = jnp.concatenate([z, jnp.zeros((1, emb), F32), z[1:][::-1]], axis=0)
    z = jnp.pad(z, ((0, 0), (0, LANES - emb)))
    w1p = jnp.pad(w1, ((0, LANES - emb), (0, 0)))
    max_decay = math.log(DECAY_TARGET) / FAST_DECAY_PCT
    min_decay = math.log(DECAY_TARGET) / SLOW_DECAY_PCT
    deltas = jnp.abs(jnp.linspace(min_decay, max_decay, c, dtype=F32)).reshape(1, c)
    full = lambda a: pl.BlockSpec(a.shape, lambda i: (0,) * a.ndim)
    half_blocks = seq_len // tl
    pre = [w1p, b1.reshape(1, fw), w2, b2.reshape(1, fw), w3, b3.reshape(1, fw)]
    post = [freq.reshape(1, fw), deltas]
    kern = functools.partial(_filter_kernel, seq_len=seq_len)
    return pl.pallas_call(
        kern,
        grid=(2 * half_blocks,),
        in_specs=([pl.BlockSpec((tl, LANES), lambda i: (i, 0))] + [full(a) for a in pre]
                  + [pl.BlockSpec((fw, c), lambda i: (0, i // half_blocks))]
                  + [full(a) for a in post]),
        out_specs=pl.BlockSpec((tl, c), lambda i: (i, 0)),
        out_shape=jax.ShapeDtypeStruct((2 * seq_len, c), BF16),
        compiler_params=_cparams("parallel"),
        name="hyena_filters",
    )(z, *pre, w4, *post)


def _dft_outer_kernel(w_ref, x_ref, or_ref, oi_ref, xs_ref, os_ref):
    k, tm2, tc = x_ref.shape
    n1 = or_ref.shape[0]
    groups = tc // LANES
    x = x_ref[...].astype(F32).reshape(k * tm2, tc)
    for g in range(groups):
        xs_ref[g] = x[:, g * LANES:(g + 1) * LANES]
    w = w_ref[...]
    for j in range(tm2):
        xj = jnp.concatenate([xs_ref[g, pl.ds(j, k, stride=tm2), :] for g in range(groups)], axis=1)
        res = jnp.dot(w, xj.astype(BF16), preferred_element_type=F32)
        for g in range(groups):
            os_ref[g, pl.ds(j, 2 * n1, stride=tm2), :] = res[:, g * LANES:(g + 1) * LANES]
    for g in range(groups):
        o = os_ref[g].astype(or_ref.dtype).reshape(2 * n1, tm2, LANES)
        or_ref[:, :, g * LANES:(g + 1) * LANES] = o[:n1]
        oi_ref[:, :, g * LANES:(g + 1) * LANES] = o[n1:]


def _dft_outer(w, x4, tm2, tc):
    g, k, n2, c = x4.shape
    n1 = w.shape[0] // 2
    out = jax.ShapeDtypeStruct((g, n1, n2, c), BF16)
    return pl.pallas_call(
        _dft_outer_kernel,
        grid=(g, n2 // tm2, c // tc),
        in_specs=[pl.BlockSpec(w.shape, lambda a, i, j: (0, 0)),
                  pl.BlockSpec((None, k, tm2, tc), lambda a, i, j: (a, 0, i, j))],
        out_specs=[pl.BlockSpec((None, n1, tm2, tc), lambda a, i, j: (a, 0, i, j)),
                   pl.BlockSpec((None, n1, tm2, tc), lambda a, i, j: (a, 0, i, j))],
        out_shape=[out, out],
        scratch_shapes=[pltpu.VMEM((tc // LANES, k * tm2, LANES), F32),
                        pltpu.VMEM((tc // LANES, 2 * n1 * tm2, LANES), F32)],
        compiler_params=_cparams("parallel", "parallel", "parallel"),
        name="dft_outer",
    )(w, x4)


def _spectrum_kernel(ar_ref, ai_ref, wf_ref, o_ref, s_ref, *, scale):
    n2 = ar_ref.shape[0]
    s_ref[0:n2, :] = ar_ref[...]
    s_ref[n2:2 * n2, :] = ai_ref[...]
    o_ref[...] = jnp.dot(wf_ref[...], s_ref[...], preferred_element_type=F32) * scale


def _filter_spectrum(ar, ai, wf, scale):
    n1, n2, c = ar.shape
    r = 2 * n2
    kern = functools.partial(_spectrum_kernel, scale=scale)
    slab = pl.BlockSpec((None, n2, c), lambda k: (k, 0, 0))
    return pl.pallas_call(
        kern,
        grid=(n1,),
        in_specs=[slab, slab, pl.BlockSpec((None, r, r), lambda k: (k, 0, 0))],
        out_specs=pl.BlockSpec((None, r, c), lambda k: (k, 0, 0)),
        out_shape=jax.ShapeDtypeStruct((n1, r, c), F32),
        scratch_shapes=[pltpu.VMEM((r, c), BF16)],
        compiler_params=_cparams("parallel"),
        name="filter_spectrum",
    )(ar, ai, wf)


def _fft_mid_kernel(ar_ref, ai_ref, h_ref, wf_ref, wi_ref, zr_ref, zi_ref, s_ref):
    batch, n2, _ = ar_ref.shape
    h = h_ref[...]
    hr, hi = h[:n2], h[n2:]
    for b in range(batch):
        s_ref[b, 0:n2, :] = ar_ref[b]
        s_ref[b, n2:2 * n2, :] = ai_ref[b]
        x = jnp.dot(wf_ref[...], s_ref[b], preferred_element_type=F32)
        xr, xi = x[:n2], x[n2:]
        s_ref[b, 0:n2, :] = (xr * hr - xi * hi).astype(s_ref.dtype)
        s_ref[b, n2:2 * n2, :] = (xr * hi + xi * hr).astype(s_ref.dtype)
        z = jnp.dot(wi_ref[...], s_ref[b], preferred_element_type=F32)
        zr_ref[b] = z[:n2].astype(zr_ref.dtype)
        zi_ref[b] = z[n2:].astype(zi_ref.dtype)


def _fft_mid(ar, ai, hspec, wf, wi):
    batch, n1, n2, c = ar.shape
    r = 2 * n2
    slab = pl.BlockSpec((batch, None, n2, c), lambda k: (0, k, 0, 0))
    out = jax.ShapeDtypeStruct((batch, n1, n2, c), BF16)
    return pl.pallas_call(
        _fft_mid_kernel,
        grid=(n1,),
        in_specs=[slab, slab,
                  pl.BlockSpec((None, r, c), lambda k: (k, 0, 0)),
                  pl.BlockSpec((None, r, r), lambda k: (k, 0, 0)),
                  pl.BlockSpec((None, r, r), lambda k: (k, 0, 0))],
        out_specs=[slab, slab],
        out_shape=[out, out],
        scratch_shapes=[pltpu.VMEM((batch, r, c), BF16)],
        compiler_params=_cparams("parallel"),
        name="fft_mid",
    )(ar, ai, hspec, wf, wi)


def _fft_out_kernel(w_ref, zr_ref, zi_ref, u_ref, x0_ref, b_ref, o_ref, rs_ref, is_ref, ys_ref):
    n1, tm2, tc = zr_ref.shape
    n1h = o_ref.shape[0]
    groups = tc // LANES
    zr = zr_ref[...].astype(F32).reshape(n1 * tm2, tc)
    zi = zi_ref[...].astype(F32).reshape(n1 * tm2, tc)
    for g in range(groups):
        rs_ref[g] = zr[:, g * LANES:(g + 1) * LANES]
        is_ref[g] = zi[:, g * LANES:(g + 1) * LANES]
    w = w_ref[...]
    for j in range(tm2):
        parts = [jnp.concatenate([rs_ref[g, pl.ds(j, n1, stride=tm2), :] for g in range(groups)], axis=1),
                 jnp.concatenate([is_ref[g, pl.ds(j, n1, stride=tm2), :] for g in range(groups)], axis=1)]
        if w.shape[1] > 2 * n1:
            parts.append(jnp.zeros((w.shape[1] - 2 * n1, tc), F32))
        zj = jnp.concatenate(parts, axis=0)
        y = jnp.dot(w, zj.astype(BF16), preferred_element_type=F32)
        for g in range(groups):
            ys_ref[g, pl.ds(j, n1h, stride=tm2), :] = y[:, g * LANES:(g + 1) * LANES]
    for g in range(groups):
        sl = slice(g * LANES, (g + 1) * LANES)
        y = ys_ref[g].reshape(n1h, tm2, LANES)
        u = u_ref[:, :, sl].astype(F32)
        o_ref[:, :, sl] = ((y + u * b_ref[:, sl]) * x0_ref[:, :, sl].astype(F32)).astype(o_ref.dtype)


def _fft_out(w, zr, zi, u4, x04, bias, tm2, tc):
    batch, n1, n2, c = zr.shape
    n1h = w.shape[0]
    zspec = pl.BlockSpec((None, n1, tm2, tc), lambda a, i, j: (a, 0, i, j))
    uspec = pl.BlockSpec((None, n1h, tm2, tc), lambda a, i, j: (a, 0, i, j))
    return pl.pallas_call(
        _fft_out_kernel,
        grid=(batch, n2 // tm2, c // tc),
        in_specs=[pl.BlockSpec(w.shape, lambda a, i, j: (0, 0)), zspec, zspec, uspec, uspec,
                  pl.BlockSpec((1, tc), lambda a, i, j: (0, j))],
        out_specs=uspec,
        out_shape=jax.ShapeDtypeStruct((batch, n1h, n2, c), BF16),
        scratch_shapes=[pltpu.VMEM((tc // LANES, n1 * tm2, LANES), F32),
                        pltpu.VMEM((tc // LANES, n1 * tm2, LANES), F32),
                        pltpu.VMEM((tc // LANES, n1h * tm2, LANES), F32)],
        compiler_params=_cparams("parallel", "parallel", "parallel"),
        name="fft_out",
    )(w, zr, zi, u4, x04, bias.reshape(1, c))


def _dft_tables(seq_len):
    n = 2 * seq_len
    n2 = DFT_INNER
    n1 = n // n2
    n1h = n1 // 2
    nk = n1h + 8
    k1 = np.arange(nk)[:, None]
    m1 = np.arange(n1)[None, :]
    ang1 = 2.0 * np.pi * k1 * m1 / n1
    w1_full = np.concatenate([np.cos(ang1), -np.sin(ang1)], axis=0)
    w1_half = w1_full[:, :n1h]
    weight = np.where((np.arange(nk) == 0) | (np.arange(nk) == n1h), 1.0, 2.0) * (np.arange(nk) <= n1h)
    ang1i = ang1.T[:n1h]
    w1_inv = np.concatenate([np.cos(ang1i) * weight, -np.sin(ang1i) * weight], axis=1)
    w1_inv = np.pad(w1_inv, ((0, 0), (0, -(2 * nk) % LANES)))
    k1d = jnp.arange(nk, dtype=jnp.int32)[:, None, None]
    k2d = jnp.arange(n2, dtype=jnp.int32)[None, :, None]
    m2d = jnp.arange(n2, dtype=jnp.int32)[None, None, :]
    phase = (m2d * (k1d + n1 * k2d)) % n
    ang = phase.astype(F32) * (2.0 * math.pi / n)
    c, s = jnp.cos(ang), jnp.sin(ang)
    w2_fwd = jnp.concatenate([jnp.concatenate([c, s], axis=2), jnp.concatenate([-s, c], axis=2)], axis=1)
    ct, st = jnp.swapaxes(c, 1, 2), jnp.swapaxes(s, 1, 2)
    w2_inv = jnp.concatenate([jnp.concatenate([ct, -st], axis=2), jnp.concatenate([st, ct], axis=2)], axis=1)
    bf = lambda a: jnp.asarray(a, F32).astype(BF16)
    return dict(n1=n1, n2=n2, n1h=n1h, w1_full=bf(w1_full), w1_half=bf(w1_half), w1_inv=bf(w1_inv),
                w2_fwd=w2_fwd.astype(BF16), w2_inv=w2_inv.astype(BF16))


def _hyena_long_conv(u, x0, circ, bias, batch, seq_len, tab, tm2, tc):
    c = u.shape[1]
    n1, n2, n1h = tab["n1"], tab["n2"], tab["n1h"]
    tc = _pick(c, tc)
    far, fai = _dft_outer(tab["w1_full"], circ.reshape(1, n1, n2, c), tm2, tc)
    hspec = _filter_spectrum(far[0], fai[0], tab["w2_fwd"], 1.0 / (n1 * n2))
    u4 = u.reshape(batch, n1h, n2, c)
    x04 = x0.reshape(batch, n1h, n2, c)
    ar, ai = _dft_outer(tab["w1_half"], u4, tm2, tc)
    zr, zi = _fft_mid(ar, ai, hspec, tab["w2_fwd"], tab["w2_inv"])
    y = _fft_out(tab["w1_inv"], zr, zi, u4, x04, bias, tm2, tc)
    return y.reshape(batch * seq_len, c)


def _merge_kernel(attn_ref, hy_ref, gate_ref, x_ref, wa_ref, wh_ref, wo_ref, g2_ref,
                  x1_ref, hn_ref):
    d = x_ref.shape[1]
    pa = jnp.dot(attn_ref[...], wa_ref[...], preferred_element_type=F32)
    ph = jnp.dot(hy_ref[...], wh_ref[...], preferred_element_type=F32)
    merged = gate_ref[:, :d].astype(F32) * pa + gate_ref[:, d:].astype(F32) * ph
    x1 = x_ref[...] + jnp.dot(merged.astype(BF16), wo_ref[...], preferred_element_type=F32)
    x1_ref[...] = x1
    ms = jnp.mean(x1 * x1, axis=-1, keepdims=True)
    hn_ref[...] = (x1 * lax.rsqrt(ms + NORM_EPS) * g2_ref[...]).astype(hn_ref.dtype)


def _merge(attn, hyena, gates, x, wa, wh, wo, norm2, tm):
    t, d = x.shape
    const = lambda a: pl.BlockSpec(a.shape, lambda i: (0, 0), pipeline_mode=pl.Buffered(1))
    row = lambda n: pl.BlockSpec((tm, n), lambda i: (i, 0))
    g2 = norm2.reshape(1, d)
    return pl.pallas_call(
        _merge_kernel,
        grid=(t // tm,),
        in_specs=[row(attn.shape[1]), row(hyena.shape[1]), row(gates.shape[1]), row(d),
                  const(wa), const(wh), const(wo), const(g2)],
        out_specs=[row(d), row(d)],
        out_shape=[jax.ShapeDtypeStruct((t, d), F32), jax.ShapeDtypeStruct((t, d), BF16)],
        compiler_params=_cparams("parallel"),
        name="merge_out_proj",
    )(attn, hyena, gates, x, wa, wh, wo, g2)


def _ffn_kernel(prev_ref, main_ref, next_ref, x1_ref, wg_ref, wv_ref, cg_ref, cv_ref, bg_ref, bv_ref,
                wd_ref, gf_ref, o_ref, ext_ref, acc_ref, *, blocks_per_seq):
    i = pl.program_id(0)
    j = pl.program_id(1)
    halo = prev_ref.shape[0]
    tm = main_ref.shape[0]

    @pl.when(j == 0)
    def _():
        first = (i % blocks_per_seq) == 0
        last = (i % blocks_per_seq) == blocks_per_seq - 1
        zero = jnp.zeros(prev_ref.shape, prev_ref.dtype)
        ext_ref[0:halo, :] = jnp.where(first, zero, prev_ref[...])
        ext_ref[halo:halo + tm, :] = main_ref[...]
        ext_ref[halo + tm:, :] = jnp.where(last, zero, next_ref[...])
        acc_ref[...] = jnp.zeros(acc_ref.shape, F32)

    ext = ext_ref[...]
    ug = jnp.dot(ext, wg_ref[...], preferred_element_type=F32)
    uv = jnp.dot(ext, wv_ref[...], preferred_element_type=F32)
    ug = _conv3_rows(ug, cg_ref[...], tm, halo) + bg_ref[...]
    uv = _conv3_rows(uv, cv_ref[...], tm, halo) + bv_ref[...]
    act = (ug / (1.0 + jnp.exp(-ug))) * uv
    acc_ref[...] += jnp.dot(act.astype(BF16), wd_ref[...], preferred_element_type=F32)

    @pl.when(j == pl.num_programs(1) - 1)
    def _():
        y = x1_ref[...] + acc_ref[...]
        ms = jnp.mean(y * y, axis=-1, keepdims=True)
        o_ref[...] = y * lax.rsqrt(ms + NORM_EPS) * gf_ref[...]


def _ffn(hn2, x1, w_up, conv_w, conv_b, w_down, norm_f, seq_len, tm, tf):
    t, d = x1.shape
    dff = w_down.shape[0]
    nj = dff // tf
    halo = BF16_SUBLANES
    r = tm // halo
    nb = t // halo
    kern = functools.partial(_ffn_kernel, blocks_per_seq=seq_len // tm)
    cb = conv_b.reshape(1, 2 * dff)
    return pl.pallas_call(
        kern,
        grid=(t // tm, nj),
        in_specs=[pl.BlockSpec((halo, d), lambda i, j: (jnp.maximum(i * r - 1, 0), 0)),
                  pl.BlockSpec((tm, d), lambda i, j: (i, 0)),
                  pl.BlockSpec((halo, d), lambda i, j: (jnp.minimum((i + 1) * r, nb - 1), 0)),
                  pl.BlockSpec((tm, d), lambda i, j: (i, 0)),
                  pl.BlockSpec((d, tf), lambda i, j: (0, j)),
                  pl.BlockSpec((d, tf), lambda i, j: (0, nj + j)),
                  pl.BlockSpec((3, tf), lambda i, j: (0, j)),
                  pl.BlockSpec((3, tf), lambda i, j: (0, nj + j)),
                  pl.BlockSpec((1, tf), lambda i, j: (0, j)),
                  pl.BlockSpec((1, tf), lambda i, j: (0, nj + j)),
                  pl.BlockSpec((tf, d), lambda i, j: (j, 0)),
                  pl.BlockSpec((1, d), lambda i, j: (0, 0))],
        out_specs=pl.BlockSpec((tm, d), lambda i, j: (i, 0)),
        out_shape=jax.ShapeDtypeStruct((t, d), F32),
        scratch_shapes=[pltpu.VMEM((tm + 2 * halo, d), BF16), pltpu.VMEM((tm, d), F32)],
        compiler_params=_cparams("parallel", "arbitrary"),
        name="conv_glu_mlp",
    )(hn2, hn2, hn2, x1, w_up, w_up, conv_w, conv_w, cb, cb, w_down, norm_f.reshape(1, d))


def _tiles(seq_len, d_model, d_ff):
    return dict(
        tm_norm=_pick(seq_len, 512),
        tm_proj=_pick(seq_len, 1024),
        tn_proj=2048,
        tq=_pick(seq_len, 512),
        tk=_pick(seq_len, 1024),
        tm_prep=_pick(seq_len, 512),
        tl_filt=_pick(seq_len, 512),
        tm2_dft=BF16_SUBLANES,
        tc_dft=256,
        tm_merge=_pick(seq_len, 512),
        tm_ffn=_pick(seq_len, 512),
        tf_ffn=_pick(d_ff, 512),
    )


def _encoder_layer(x, batch, seq_len, lambda_init, p, consts):
    t, d = x.shape
    tl = _tiles(seq_len, d, p["w_down"].shape[0])
    hn = _rmsnorm(x, p["norm1"], tl["tm_norm"])

    qk = _proj_rope(hn, p["w_qk"], consts["cos"], consts["sin"], seq_len, tl["tm_proj"])
    v = _proj_plain(hn, p["w_v"], tl["tm_proj"], _pick(p["w_v"].shape[1], tl["tn_proj"]))
    hy = _proj_plain(hn, p["w_hy"], tl["tm_proj"], _pick(p["w_hy"].shape[1], tl["tn_proj"]))
    gates = _proj_gate(hn, p["w_gl"], p["gate_b"], tl["tm_proj"], _pick(p["w_gl"].shape[1], tl["tn_proj"]))

    attn = _attention(qk.reshape(batch, seq_len, -1), v.reshape(batch, seq_len, -1),
                      p["lambda_q1"], p["lambda_k1"], p["lambda_q2"], p["lambda_k2"], p["subln_g"],
                      batch, seq_len, lambda_init, tl["tq"], tl["tk"]).reshape(t, ATTN_WIDTH)

    u, x0 = _hyena_prep(hy, p["in_conv_w"], p["in_conv_b"], seq_len, tl["tm_prep"])
    circ = _hyena_filters(seq_len, p["filt_w1"], p["filt_b1"], p["filt_w2"], p["filt_b2"],
                          p["filt_w3"], p["filt_b3"], p["filt_w4"], p["filt_freq"], tl["tl_filt"])
    hyena = _hyena_long_conv(u, x0, circ, p["hyena_bias"], batch, seq_len, consts["dft"],
                             tl["tm2_dft"], tl["tc_dft"])

    x1, hn2 = _merge(attn, hyena, gates, x, p["w_attn_out"], p["w_hyena_out"], p["w_out"], p["norm2"],
                     tl["tm_merge"])
    return x1, hn2, tl


def kernel(x_prompt, x_sample, norm1, w_in, in_conv_w, in_conv_b, gate_b, lambda_q1, lambda_k1, lambda_q2, lambda_k2, subln_g, filt_w1, filt_b1, filt_w2, filt_b2, filt_w3, filt_b3, filt_w4, filt_freq, hyena_bias, w_attn_out, w_hyena_out, w_out, norm2, w_up, ffn_conv_w, ffn_conv_b, w_down, norm_f):
    depth = norm1.shape[0]
    assert depth == 1, "this kernel implements the single-layer trunk"
    d_model = x_prompt.shape[-1]
    hyena_width = hyena_bias.shape[-1]
    o1 = ATTN_QK_WIDTH
    o2 = o1 + ATTN_QK_WIDTH
    o3 = o2 + ATTN_WIDTH
    o4 = o3 + 3 * hyena_width

    layers = []
    for i in range(depth):
        wi = w_in[i].astype(BF16)
        layers.append(dict(
            norm1=norm1[i], w_qk=wi[:, :o2], w_v=wi[:, o2:o3], w_hy=wi[:, o3:o4], w_gl=wi[:, o4:],
            in_conv_w=in_conv_w[i], in_conv_b=in_conv_b[i], gate_b=gate_b[i],
            lambda_q1=lambda_q1[i], lambda_k1=lambda_k1[i], lambda_q2=lambda_q2[i], lambda_k2=lambda_k2[i],
            subln_g=subln_g[i],
            filt_w1=filt_w1[i], filt_b1=filt_b1[i], filt_w2=filt_w2[i], filt_b2=filt_b2[i],
            filt_w3=filt_w3[i], filt_b3=filt_b3[i], filt_w4=filt_w4[i], filt_freq=filt_freq[i],
            hyena_bias=hyena_bias[i],
            w_attn_out=w_attn_out[i].astype(BF16), w_hyena_out=w_hyena_out[i].astype(BF16),
            w_out=w_out[i].astype(BF16), norm2=norm2[i],
            w_up=w_up[i].astype(BF16), ffn_conv_w=ffn_conv_w[i], ffn_conv_b=ffn_conv_b[i],
            w_down=w_down[i].astype(BF16)))

    def trunk(x3):
        batch, seq_len, _ = x3.shape
        cos, sin = _rope_tables(seq_len)
        consts = dict(cos=cos, sin=sin, dft=_dft_tables(seq_len))
        x = x3.reshape(batch * seq_len, d_model)
        lambda_init = 0.8 - 0.6 * math.exp(-0.3 * 0)
        p = layers[0]
        x1, hn2, tl = _encoder_layer(x, batch, seq_len, lambda_init, p, consts)
        y = _ffn(hn2, x1, p["w_up"], p["ffn_conv_w"], p["ffn_conv_b"], p["w_down"], norm_f, seq_len,
                 tl["tm_ffn"], tl["tf_ffn"])
        return y.reshape(batch, seq_len, d_model)

    return (trunk(x_prompt), trunk(x_sample))
```

```python
import functools
import math

import numpy as np
import jax
import jax.numpy as jnp
from jax import lax
from jax.experimental import pallas as pl
from jax.experimental.pallas import tpu as pltpu

ATTN_HEADS = 8
ATTN_QK_DIM = 64
ATTN_V_DIM = 2 * ATTN_QK_DIM
ATTN_QK_WIDTH = ATTN_HEADS * 2 * ATTN_QK_DIM
ATTN_WIDTH = ATTN_HEADS * ATTN_V_DIM
HYENA_EMB_DIM = 33
HYENA_BANDS = (HYENA_EMB_DIM - 1) // 2
DECAY_TARGET = 1e-2
FAST_DECAY_PCT = 0.3
SLOW_DECAY_PCT = 1.5
ROPE_THETA = 10000.0
NORM_EPS = 1e-6

LANES = 128
BF16_SUBLANES = 16
VMEM_LIMIT_BYTES = 56 * 1024 * 1024

DFT_INNER = LANES

F32 = jnp.float32
BF16 = jnp.bfloat16


def _cparams(*sem):
    return pltpu.CompilerParams(dimension_semantics=sem, vmem_limit_bytes=VMEM_LIMIT_BYTES)


def _pick(n, pref):
    t = min(n, pref)
    while n % t:
        t //= 2
    return t


def _proj_plain_kernel(h_ref, w_ref, o_ref):
    o_ref[...] = jnp.dot(h_ref[...], w_ref[...], preferred_element_type=F32).astype(o_ref.dtype)


def _proj_plain(h, w, tm, tn):
    t, d = h.shape
    n = w.shape[1]
    return pl.pallas_call(
        _proj_plain_kernel,
        grid=(t // tm, n // tn),
        in_specs=[pl.BlockSpec((tm, d), lambda i, j: (i, 0)),
                  pl.BlockSpec((d, tn), lambda i, j: (0, j))],
        out_specs=pl.BlockSpec((tm, tn), lambda i, j: (i, j)),
        out_shape=jax.ShapeDtypeStruct((t, n), BF16),
        compiler_params=_cparams("parallel", "arbitrary"),
        name="proj_plain",
    )(h, w)


def _proj_gate_kernel(h_ref, w_ref, b_ref, o_ref):
    z = jnp.dot(h_ref[...], w_ref[...], preferred_element_type=F32) + b_ref[...]
    o_ref[...] = (1.0 / (1.0 + jnp.exp(-z))).astype(o_ref.dtype)


def _proj_gate(h, w, b, tm, tn):
    t, d = h.shape
    n = w.shape[1]
    return pl.pallas_call(
        _proj_gate_kernel,
        grid=(t // tm, n // tn),
        in_specs=[pl.BlockSpec((tm, d), lambda i, j: (i, 0)),
                  pl.BlockSpec((d, tn), lambda i, j: (0, j)),
                  pl.BlockSpec((1, tn), lambda i, j: (0, j))],
        out_specs=pl.BlockSpec((tm, tn), lambda i, j: (i, j)),
        out_shape=jax.ShapeDtypeStruct((t, n), BF16),
        compiler_params=_cparams("parallel", "arbitrary"),
        name="proj_gate",
    )(h, w, b.reshape(1, n))


def _proj_rope_kernel(x_ref, g_ref, w_ref, cos_ref, sin_ref, o_ref, hn_ref, *, q_scale):
    @pl.when(pl.program_id(1) == 0)
    def _():
        x = x_ref[...]
        ms = jnp.mean(x * x, axis=-1, keepdims=True)
        hn_ref[...] = (x * lax.rsqrt(ms + NORM_EPS) * g_ref[...]).astype(hn_ref.dtype)

    acc = jnp.dot(hn_ref[...], w_ref[...], preferred_element_type=F32)
    cos = cos_ref[...]
    sin = sin_ref[...]
    lane = lax.broadcasted_iota(jnp.int32, cos.shape, 1)
    first_half = (lane % ATTN_QK_DIM) < (ATTN_QK_DIM // 2)
    scale = jnp.where(pl.program_id(1) == 0, q_scale, 1.0).astype(F32)
    half = ATTN_QK_DIM // 2
    for g in range(acc.shape[1] // LANES):
        x = acc[:, g * LANES:(g + 1) * LANES]
        up = pltpu.roll(x, LANES - half, 1)
        down = pltpu.roll(x, half, 1)
        rot = jnp.where(first_half, up, down)
        o_ref[:, g * LANES:(g + 1) * LANES] = ((x * cos + rot * sin) * scale).astype(o_ref.dtype)


def _proj_rope(x, g, w, cos, sin, seq_len, tm):
    t, d = x.shape
    n = w.shape[1]
    tn = ATTN_QK_WIDTH
    nblk = seq_len // tm
    kern = functools.partial(_proj_rope_kernel, q_scale=ATTN_QK_DIM ** -0.5 * math.log2(math.e))
    return pl.pallas_call(
        kern,
        grid=(t // tm, n // tn),
        in_specs=[pl.BlockSpec((tm, d), lambda i, j: (i, 0)),
                  pl.BlockSpec((1, d), lambda i, j: (0, 0)),
                  pl.BlockSpec((d, tn), lambda i, j: (0, j)),
                  pl.BlockSpec((tm, LANES), lambda i, j: (i % nblk, 0)),
                  pl.BlockSpec((tm, LANES), lambda i, j: (i % nblk, 0))],
        out_specs=[pl.BlockSpec((tm, tn), lambda i, j: (i, j)),
                   pl.BlockSpec((tm, d), lambda i, j: (i, 0))],
        out_shape=[jax.ShapeDtypeStruct((t, n), BF16), jax.ShapeDtypeStruct((t, d), BF16)],
        compiler_params=_cparams("parallel", "arbitrary"),
        name="proj_rope",
    )(x, g.reshape(1, d), w, cos, sin)


def _rope_tables(seq_len):
    d = ATTN_QK_DIM
    inv = ROPE_THETA ** (-jnp.arange(0, d, 2, dtype=F32) / d)
    ang = jnp.arange(seq_len, dtype=F32)[:, None] * inv[None]
    ang = jnp.concatenate([ang, ang], -1)
    sign = jnp.concatenate([-jnp.ones((d // 2,), F32), jnp.ones((d // 2,), F32)])
    cos = jnp.cos(ang)
    sin = jnp.sin(ang) * sign
    reps = LANES // d
    return jnp.tile(cos, (1, reps)), jnp.tile(sin, (1, reps))


def _attn_kernel(q_ref, k_ref, v_ref, lq1_ref, lk1_ref, lq2_ref, lk2_ref, g_ref, o_ref,
                 sa0_ref, sb0_ref, sa1_ref, sb1_ref, m0_ref, m1_ref, acc0_ref, acc1_ref,
                 *, tq, tk, lambda_init):
    seq = k_ref.shape[0]
    n_chunks = seq // tk
    n_tiles = seq // tq
    lane = lax.broadcasted_iota(jnp.int32, (tq, LANES), 1)
    ones_col = (lax.broadcasted_iota(jnp.int32, (tk, LANES), 1) == 0).astype(BF16)
    nt = (((1,), (1,)), ((), ()))
    score_refs = ((sa0_ref, sb0_ref), (sa1_ref, sb1_ref))
    lam = (jnp.exp(jnp.sum(lq1_ref[...] * lk1_ref[...], axis=1, keepdims=True))
           - jnp.exp(jnp.sum(lq2_ref[...] * lk2_ref[...], axis=1, keepdims=True))
           + lambda_init)

    def tile_rows(tile):
        start = tile * tq if isinstance(tile, int) else pl.multiple_of(tile * tq, tq)
        return pl.ds(start, tq)

    def split_maps(tile):
        q = q_ref[tile_rows(tile), :]
        zero = jnp.zeros_like(q)
        return jnp.where(lane < ATTN_QK_DIM, q, zero), jnp.where(lane >= ATTN_QK_DIM, q, zero)

    def scores(qmaps, j, slot):
        kc = k_ref[j * tk:(j + 1) * tk, :]
        sa_ref, sb_ref = score_refs[slot]
        sa_ref[...] = lax.dot_general(qmaps[0], kc, nt, preferred_element_type=F32)
        sb_ref[...] = lax.dot_general(qmaps[1], kc, nt, preferred_element_type=F32)

    def one_map(s_ref, vext, m_ref, acc_ref, first):
        s = s_ref[...]
        m_cur = jnp.max(s, axis=1, keepdims=True)
        if first:
            m_new = jnp.broadcast_to(m_cur, m_ref.shape)
        else:
            m_old = m_ref[...]
            m_new = jnp.maximum(m_old, m_cur)
        p = jnp.concatenate(
            [jnp.exp2(s[:, c * LANES:(c + 1) * LANES] - m_new) for c in range(tk // LANES)],
            axis=1).astype(BF16)
        pv = jnp.dot(p, vext, preferred_element_type=F32)
        if first:
            acc_ref[...] = pv
        else:
            alpha = jnp.exp2(m_old - m_new)
            acc_ref[...] = jnp.concatenate([alpha, alpha], axis=1) * acc_ref[...] + pv
        m_ref[...] = m_new

    def weighted_values(j, slot):
        vext = jnp.concatenate([v_ref[j * tk:(j + 1) * tk, :], ones_col], axis=1)
        sa_ref, sb_ref = score_refs[slot]
        one_map(sa_ref, vext, m0_ref, acc0_ref, j == 0)
        one_map(sb_ref, vext, m1_ref, acc1_ref, j == 0)

    def finish(tile):
        a0 = acc0_ref[...]
        a1 = acc1_ref[...]
        o0 = a0[:, :LANES] / a0[:, LANES:LANES + 1]
        o1 = a1[:, :LANES] / a1[:, LANES:LANES + 1]
        o = o0 - lam * o1
        ms = jnp.mean(o * o, axis=-1, keepdims=True)
        o = o * lax.rsqrt(ms + NORM_EPS) * g_ref[...]
        o_ref[tile_rows(tile), :] = (o * (1.0 - lambda_init)).astype(o_ref.dtype)

    scores(split_maps(0), 0, 0)
    acc0_ref[...] = jnp.ones(acc0_ref.shape, F32)
    acc1_ref[...] = jnp.ones(acc1_ref.shape, F32)

    def tile_body(tile, carry):
        qmaps = split_maps(tile)
        for j in range(0, n_chunks, 2):
            scores(qmaps, j + 1, 1)
            if j == 0:
                finish(jnp.maximum(tile - 1, 0))
            weighted_values(j, 0)
            if j + 2 < n_chunks:
                scores(qmaps, j + 2, 0)
            else:
                scores(split_maps(jnp.minimum(tile + 1, n_tiles - 1)), 0, 0)
            weighted_values(j + 1, 1)
        return carry

    lax.fori_loop(0, n_tiles, tile_body, 0)
    finish(n_tiles - 1)


def _attention(qk, v, lq1, lk1, lq2, lk2, subln_g, batch, seq_len, lambda_init, tq, tk):
    assert (seq_len // tk) % 2 == 0, "key chunks are processed in pairs"
    kern = functools.partial(_attn_kernel, tq=tq, tk=tk, lambda_init=lambda_init)
    small = pl.BlockSpec((1, ATTN_QK_DIM), lambda b, h: (0, 0))
    return pl.pallas_call(
        kern,
        grid=(batch, ATTN_HEADS),
        in_specs=[pl.BlockSpec((None, seq_len, LANES), lambda b, h: (b, 0, h)),
                  pl.BlockSpec((None, seq_len, LANES), lambda b, h: (b, 0, ATTN_HEADS + h)),
                  pl.BlockSpec((None, seq_len, LANES), lambda b, h: (b, 0, h)),
                  small, small, small, small,
                  pl.BlockSpec((1, LANES), lambda b, h: (0, 0))],
        out_specs=pl.BlockSpec((None, seq_len, LANES), lambda b, h: (b, 0, h)),
        out_shape=jax.ShapeDtypeStruct((batch, seq_len, ATTN_WIDTH), BF16),
        scratch_shapes=[pltpu.VMEM((tq, tk), F32)] * 4
                       + [pltpu.VMEM((tq, LANES), F32), pltpu.VMEM((tq, LANES), F32),
                          pltpu.VMEM((tq, 2 * LANES), F32), pltpu.VMEM((tq, 2 * LANES), F32)],
        compiler_params=_cparams("parallel", "parallel"),
        name="diff_attention",
    )(qk, qk, v, lq1.reshape(1, -1), lk1.reshape(1, -1), lq2.reshape(1, -1), lk2.reshape(1, -1),
      subln_g.reshape(1, -1))


def _conv3_rows(e, w, n_rows, halo):
    tot = e.shape[0]
    prev = pltpu.roll(e, 1, 0)
    nxt = pltpu.roll(e, tot - 1, 0)
    c = prev * w[0:1, :] + e * w[1:2, :] + nxt * w[2:3, :]
    return c[halo:halo + n_rows, :]


def _hyena_prep_kernel(prev_ref, main_ref, next_ref, w_ref, b_ref, u_ref, x0_ref, *, blocks_per_seq):
    i = pl.program_id(0)
    first = (i % blocks_per_seq) == 0
    last = (i % blocks_per_seq) == blocks_per_seq - 1
    halo = prev_ref.shape[0]
    tm = main_ref.shape[0]
    prev = jnp.where(first, 0.0, prev_ref[...].astype(F32))
    nxt = jnp.where(last, 0.0, next_ref[...].astype(F32))
    e = jnp.concatenate([prev, main_ref[...].astype(F32), nxt], axis=0)
    c = _conv3_rows(e, w_ref[...], tm, halo) + b_ref[...]
    width = u_ref.shape[1]
    x0 = c[:, :width]
    x1 = c[:, width:2 * width]
    hv = c[:, 2 * width:]
    u_ref[...] = (hv * x1).astype(u_ref.dtype)
    x0_ref[...] = x0.astype(x0_ref.dtype)


def _hyena_prep(hy, conv_w, conv_b, seq_len, tm):
    t, n = hy.shape
    width = n // 3
    halo = BF16_SUBLANES
    r = tm // halo
    nb = t // halo
    kern = functools.partial(_hyena_prep_kernel, blocks_per_seq=seq_len // tm)
    return pl.pallas_call(
        kern,
        grid=(t // tm,),
        in_specs=[pl.BlockSpec((halo, n), lambda i: (jnp.maximum(i * r - 1, 0), 0)),
                  pl.BlockSpec((tm, n), lambda i: (i, 0)),
                  pl.BlockSpec((halo, n), lambda i: (jnp.minimum((i + 1) * r, nb - 1), 0)),
                  pl.BlockSpec((3, n), lambda i: (0, 0)),
                  pl.BlockSpec((1, n), lambda i: (0, 0))],
        out_specs=[pl.BlockSpec((tm, width), lambda i: (i, 0)),
                   pl.BlockSpec((tm, width), lambda i: (i, 0))],
        out_shape=[jax.ShapeDtypeStruct((t, width), BF16),
                   jax.ShapeDtypeStruct((t, width), BF16)],
        compiler_params=_cparams("parallel"),
        name="hyena_prep",
    )(hy, hy, hy, conv_w, conv_b.reshape(1, n))


def _filter_kernel(z_ref, w1_ref, b1_ref, w2_ref, b2_ref, w3_ref, b3_ref, w4_ref, f_ref, d_ref, o_ref,
                   *, seq_len):
    hp = lax.Precision.HIGHEST
    z = z_ref[...]
    f = f_ref[...]
    h = jnp.sin(f * (jnp.dot(z, w1_ref[...], precision=hp, preferred_element_type=F32) + b1_ref[...]))
    h = jnp.sin(f * (jnp.dot(h, w2_ref[...], precision=hp, preferred_element_type=F32) + b2_ref[...]))
    h = jnp.sin(f * (jnp.dot(h, w3_ref[...], precision=hp, preferred_element_type=F32) + b3_ref[...]))
    h = jnp.dot(h, w4_ref[...], precision=hp, preferred_element_type=F32)
    t = z[:, 0:1]
    h = h * jnp.exp(-t * d_ref[...])
    row = pl.program_id(0) * z.shape[0] + lax.broadcasted_iota(jnp.int32, (z.shape[0], 1), 0)
    o_ref[...] = jnp.where(row == seq_len, 0.0, h).astype(o_ref.dtype)


def _hyena_filters(seq_len, w1, b1, w2, b2, w3, b3, w4, freq, tl):
    fw = w1.shape[1]
    c = w4.shape[1] // 2
    t = jnp.linspace(0.0, 1.0, seq_len, dtype=F32)[:, None]
    w = 2.0 * math.pi * jnp.arange(seq_len, dtype=F32)[:, None] / seq_len
    f = jnp.linspace(1e-4, HYENA_BANDS - 1, HYENA_BANDS, dtype=F32)[None]
    z = jnp.concatenate([t, jnp.cos(f * w), -jnp.sin(f * w)], -1)
    emb = z.shape[1]
    z = jnp.concatenate([z, jnp.zeros((1, emb), F32), z[1:][::-1]], axis=0)
    z = jnp.pad(z, ((0, 0), (0, LANES - emb)))
    w1p = jnp.pad(w1, ((0, LANES - emb), (0, 0)))
    max_decay = math.log(DECAY_TARGET) / FAST_DECAY_PCT
    min_decay = math.log(DECAY_TARGET) / SLOW_DECAY_PCT
    deltas = jnp.abs(jnp.linspace(min_decay, max_decay, c, dtype=F32)).reshape(1, c)
    full = lambda a: pl.BlockSpec(a.shape, lambda i: (0,) * a.ndim)
    half_blocks = seq_len // tl
    pre = [w1p, b1.reshape(1, fw), w2, b2.reshape(1, fw), w3, b3.reshape(1, fw)]
    post = [freq.reshape(1, fw), deltas]
    kern = functools.partial(_filter_kernel, seq_len=seq_len)
    return pl.pallas_call(
        kern,
        grid=(2 * half_blocks,),
        in_specs=([pl.BlockSpec((tl, LANES), lambda i: (i, 0))] + [full(a) for a in pre]
                  + [pl.BlockSpec((fw, c), lambda i: (0, i // half_blocks))]
                  + [full(a) for a in post]),
        out_specs=pl.BlockSpec((tl, c), lambda i: (i, 0)),
        out_shape=jax.ShapeDtypeStruct((2 * seq_len, c), BF16),
        compiler_params=_cparams("parallel"),
        name="hyena_filters",
    )(z, *pre, w4, *post)


def _dft_outer_kernel(w_ref, x_ref, or_ref, oi_ref, xs_ref, os_ref):
    k, tm2, tc = x_ref.shape
    n1 = or_ref.shape[0]
    groups = tc // LANES
    x = x_ref[...].astype(F32).reshape(k * tm2, tc)
    for g in range(groups):
        xs_ref[g] = x[:, g * LANES:(g + 1) * LANES]
    w = w_ref[...]
    for j in range(tm2):
        xj = jnp.concatenate([xs_ref[g, pl.ds(j, k, stride=tm2), :] for g in range(groups)], axis=1)
        res = jnp.dot(w, xj.astype(BF16), preferred_element_type=F32)
        for g in range(groups):
            os_ref[g, pl.ds(j, 2 * n1, stride=tm2), :] = res[:, g * LANES:(g + 1) * LANES]
    for g in range(groups):
        o = os_ref[g].astype(or_ref.dtype).reshape(2 * n1, tm2, LANES)
        or_ref[:, :, g * LANES:(g + 1) * LANES] = o[:n1]
        oi_ref[:, :, g * LANES:(g + 1) * LANES] = o[n1:]


def _dft_outer(w, x4, tm2, tc):
    g, k, n2, c = x4.shape
    n1 = w.shape[0] // 2
    out = jax.ShapeDtypeStruct((g, n1, n2, c), BF16)
    return pl.pallas_call(
        _dft_outer_kernel,
        grid=(g, n2 // tm2, c // tc),
        in_specs=[pl.BlockSpec(w.shape, lambda a, i, j: (0, 0)),
                  pl.BlockSpec((None, k, tm2, tc), lambda a, i, j: (a, 0, i, j))],
        out_specs=[pl.BlockSpec((None, n1, tm2, tc), lambda a, i, j: (a, 0, i, j)),
                   pl.BlockSpec((None, n1, tm2, tc), lambda a, i, j: (a, 0, i, j))],
        out_shape=[out, out],
        scratch_shapes=[pltpu.VMEM((tc // LANES, k * tm2, LANES), F32),
                        pltpu.VMEM((tc // LANES, 2 * n1 * tm2, LANES), F32)],
        compiler_params=_cparams("parallel", "parallel", "parallel"),
        name="dft_outer",
    )(w, x4)


def _spectrum_kernel(ar_ref, ai_ref, wf_ref, o_ref, s_ref, *, scale):
    n2 = ar_ref.shape[0]
    s_ref[0:n2, :] = ar_ref[...]
    s_ref[n2:2 * n2, :] = ai_ref[...]
    o_ref[...] = jnp.dot(wf_ref[...], s_ref[...], preferred_element_type=F32) * scale


def _filter_spectrum(ar, ai, wf, scale):
    n1, n2, c = ar.shape
    r = 2 * n2
    kern = functools.partial(_spectrum_kernel, scale=scale)
    slab = pl.BlockSpec((None, n2, c), lambda k: (k, 0, 0))
    return pl.pallas_call(
        kern,
        grid=(n1,),
        in_specs=[slab, slab, pl.BlockSpec((None, r, r), lambda k: (k, 0, 0))],
        out_specs=pl.BlockSpec((None, r, c), lambda k: (k, 0, 0)),
        out_shape=jax.ShapeDtypeStruct((n1, r, c), F32),
        scratch_shapes=[pltpu.VMEM((r, c), BF16)],
        compiler_params=_cparams("parallel"),
        name="filter_spectrum",
    )(ar, ai, wf)


def _fft_mid_kernel(ar_ref, ai_ref, h_ref, wf_ref, wi_ref, zr_ref, zi_ref, s_ref):
    batch, n2, _ = ar_ref.shape
    h = h_ref[...]
    hr, hi = h[:n2], h[n2:]
    for b in range(batch):
        s_ref[b, 0:n2, :] = ar_ref[b]
        s_ref[b, n2:2 * n2, :] = ai_ref[b]
        x = jnp.dot(wf_ref[...], s_ref[b], preferred_element_type=F32)
        xr, xi = x[:n2], x[n2:]
        s_ref[b, 0:n2, :] = (xr * hr - xi * hi).astype(s_ref.dtype)
        s_ref[b, n2:2 * n2, :] = (xr * hi + xi * hr).astype(s_ref.dtype)
        z = jnp.dot(wi_ref[...], s_ref[b], preferred_element_type=F32)
        zr_ref[b] = z[:n2].astype(zr_ref.dtype)
        zi_ref[b] = z[n2:].astype(zi_ref.dtype)


def _fft_mid(ar, ai, hspec, wf, wi):
    batch, n1, n2, c = ar.shape
    r = 2 * n2
    slab = pl.BlockSpec((batch, None, n2, c), lambda k: (0, k, 0, 0))
    out = jax.ShapeDtypeStruct((batch, n1, n2, c), BF16)
    return pl.pallas_call(
        _fft_mid_kernel,
        grid=(n1,),
        in_specs=[slab, slab,
                  pl.BlockSpec((None, r, c), lambda k: (k, 0, 0)),
                  pl.BlockSpec((None, r, r), lambda k: (k, 0, 0)),
                  pl.BlockSpec((None, r, r), lambda k: (k, 0, 0))],
        out_specs=[slab, slab],
        out_shape=[out, out],
        scratch_shapes=[pltpu.VMEM((batch, r, c), BF16)],
        compiler_params=_cparams("parallel"),
        name="fft_mid",
    )(ar, ai, hspec, wf, wi)


def _fft_out_kernel(w_ref, zr_ref, zi_ref, u_ref, x0_ref, b_ref, o_ref, rs_ref, is_ref, ys_ref):
    n1, tm2, tc = zr_ref.shape
    n1h = o_ref.shape[0]
    groups = tc // LANES
    zr = zr_ref[...].astype(F32).reshape(n1 * tm2, tc)
    zi = zi_ref[...].astype(F32).reshape(n1 * tm2, tc)
    for g in range(groups):
        rs_ref[g] = zr[:, g * LANES:(g + 1) * LANES]
        is_ref[g] = zi[:, g * LANES:(g + 1) * LANES]
    w = w_ref[...]
    for j in range(tm2):
        parts = [jnp.concatenate([rs_ref[g, pl.ds(j, n1, stride=tm2), :] for g in range(groups)], axis=1),
                 jnp.concatenate([is_ref[g, pl.ds(j, n1, stride=tm2), :] for g in range(groups)], axis=1)]
        if w.shape[1] > 2 * n1:
            parts.append(jnp.zeros((w.shape[1] - 2 * n1, tc), F32))
        zj = jnp.concatenate(parts, axis=0)
        y = jnp.dot(w, zj.astype(BF16), preferred_element_type=F32)
        for g in range(groups):
            ys_ref[g, pl.ds(j, n1h, stride=tm2), :] = y[:, g * LANES:(g + 1) * LANES]
    for g in range(groups):
        sl = slice(g * LANES, (g + 1) * LANES)
        y = ys_ref[g].reshape(n1h, tm2, LANES)
        u = u_ref[:, :, sl].astype(F32)
        o_ref[:, :, sl] = ((y + u * b_ref[:, sl]) * x0_ref[:, :, sl].astype(F32)).astype(o_ref.dtype)


def _fft_out(w, zr, zi, u4, x04, bias, tm2, tc):
    batch, n1, n2, c = zr.shape
    n1h = w.shape[0]
    zspec = pl.BlockSpec((None, n1, tm2, tc), lambda a, i, j: (a, 0, i, j))
    uspec = pl.BlockSpec((None, n1h, tm2, tc), lambda a, i, j: (a, 0, i, j))
    return pl.pallas_call(
        _fft_out_kernel,
        grid=(batch, n2 // tm2, c // tc),
        in_specs=[pl.BlockSpec(w.shape, lambda a, i, j: (0, 0)), zspec, zspec, uspec, uspec,
                  pl.BlockSpec((1, tc), lambda a, i, j: (0, j))],
        out_specs=uspec,
        out_shape=jax.ShapeDtypeStruct((batch, n1h, n2, c), BF16),
        scratch_shapes=[pltpu.VMEM((tc // LANES, n1 * tm2, LANES), F32),
                        pltpu.VMEM((tc // LANES, n1 * tm2, LANES), F32),
                        pltpu.VMEM((tc // LANES, n1h * tm2, LANES), F32)],
        compiler_params=_cparams("parallel", "parallel", "parallel"),
        name="fft_out",
    )(w, zr, zi, u4, x04, bias.reshape(1, c))


def _dft_tables(seq_len):
    n = 2 * seq_len
    n2 = DFT_INNER
    n1 = n // n2
    n1h = n1 // 2
    nk = n1h + 8
    k1 = np.arange(nk)[:, None]
    m1 = np.arange(n1)[None, :]
    ang1 = 2.0 * np.pi * k1 * m1 / n1
    w1_full = np.concatenate([np.cos(ang1), -np.sin(ang1)], axis=0)
    w1_half = w1_full[:, :n1h]
    weight = np.where((np.arange(nk) == 0) | (np.arange(nk) == n1h), 1.0, 2.0) * (np.arange(nk) <= n1h)
    ang1i = ang1.T[:n1h]
    w1_inv = np.concatenate([np.cos(ang1i) * weight, -np.sin(ang1i) * weight], axis=1)
    w1_inv = np.pad(w1_inv, ((0, 0), (0, -(2 * nk) % LANES)))
    k1d = jnp.arange(nk, dtype=jnp.int32)[:, None, None]
    k2d = jnp.arange(n2, dtype=jnp.int32)[None, :, None]
    m2d = jnp.arange(n2, dtype=jnp.int32)[None, None, :]
    phase = (m2d * (k1d + n1 * k2d)) % n
    ang = phase.astype(F32) * (2.0 * math.pi / n)
    c, s = jnp.cos(ang), jnp.sin(ang)
    w2_fwd = jnp.concatenate([jnp.concatenate([c, s], axis=2), jnp.concatenate([-s, c], axis=2)], axis=1)
    ct, st = jnp.swapaxes(c, 1, 2), jnp.swapaxes(s, 1, 2)
    w2_inv = jnp.concatenate([jnp.concatenate([ct, -st], axis=2), jnp.concatenate([st, ct], axis=2)], axis=1)
    bf = lambda a: jnp.asarray(a, F32).astype(BF16)
    return dict(n1=n1, n2=n2, n1h=n1h, w1_full=bf(w1_full), w1_half=bf(w1_half), w1_inv=bf(w1_inv),
                w2_fwd=w2_fwd.astype(BF16), w2_inv=w2_inv.astype(BF16))


def _hyena_long_conv(u, x0, circ, bias, batch, seq_len, tab, tm2, tc):
    c = u.shape[1]
    n1, n2, n1h = tab["n1"], tab["n2"], tab["n1h"]
    tc = _pick(c, tc)
    far, fai = _dft_outer(tab["w1_full"], circ.reshape(1, n1, n2, c), tm2, tc)
    hspec = _filter_spectrum(far[0], fai[0], tab["w2_fwd"], 1.0 / (n1 * n2))
    u4 = u.reshape(batch, n1h, n2, c)
    x04 = x0.reshape(batch, n1h, n2, c)
    ar, ai = _dft_outer(tab["w1_half"], u4, tm2, tc)
    zr, zi = _fft_mid(ar, ai, hspec, tab["w2_fwd"], tab["w2_inv"])
    y = _fft_out(tab["w1_inv"], zr, zi, u4, x04, bias, tm2, tc)
    return y.reshape(batch * seq_len, c)


def _merge_kernel(attn_ref, hy_ref, gate_ref, x_ref, wa_ref, wh_ref, wo_ref, g2_ref,
                  x1_ref, hn_ref):
    d = x_ref.shape[1]
    pa = jnp.dot(attn_ref[...], wa_ref[...], preferred_element_type=F32)
    ph = jnp.dot(hy_ref[...], wh_ref[...], preferred_element_type=F32)
    merged = gate_ref[:, :d].astype(F32) * pa + gate_ref[:, d:].astype(F32) * ph
    x1 = x_ref[...] + jnp.dot(merged.astype(BF16), wo_ref[...], preferred_element_type=F32)
    x1_ref[...] = x1
    ms = jnp.mean(x1 * x1, axis=-1, keepdims=True)
    hn_ref[...] = (x1 * lax.rsqrt(ms + NORM_EPS) * g2_ref[...]).astype(hn_ref.dtype)


def _merge(attn, hyena, gates, x, wa, wh, wo, norm2, tm):
    t, d = x.shape
    const = lambda a: pl.BlockSpec(a.shape, lambda i: (0, 0), pipeline_mode=pl.Buffered(1))
    row = lambda n: pl.BlockSpec((tm, n), lambda i: (i, 0))
    g2 = norm2.reshape(1, d)
    return pl.pallas_call(
        _merge_kernel,
        grid=(t // tm,),
        in_specs=[row(attn.shape[1]), row(hyena.shape[1]), row(gates.shape[1]), row(d),
                  const(wa), const(wh), const(wo), const(g2)],
        out_specs=[row(d), row(d)],
        out_shape=[jax.ShapeDtypeStruct((t, d), F32), jax.ShapeDtypeStruct((t, d), BF16)],
        compiler_params=_cparams("parallel"),
        name="merge_out_proj",
    )(attn, hyena, gates, x, wa, wh, wo, g2)


def _ffn_kernel(prev_ref, main_ref, next_ref, x1_ref, wg_ref, wv_ref, cg_ref, cv_ref, bg_ref, bv_ref,
                wd_ref, gf_ref, o_ref, ext_ref, acc_ref, *, blocks_per_seq):
    i = pl.program_id(0)
    j = pl.program_id(1)
    halo = prev_ref.shape[0]
    tm = main_ref.shape[0]

    @pl.when(j == 0)
    def _():
        first = (i % blocks_per_seq) == 0
        last = (i % blocks_per_seq) == blocks_per_seq - 1
        zero = jnp.zeros(prev_ref.shape, prev_ref.dtype)
        ext_ref[0:halo, :] = jnp.where(first, zero, prev_ref[...])
        ext_ref[halo:halo + tm, :] = main_ref[...]
        ext_ref[halo + tm:, :] = jnp.where(last, zero, next_ref[...])
        acc_ref[...] = jnp.zeros(acc_ref.shape, F32)

    ext = ext_ref[...]
    ug = jnp.dot(ext, wg_ref[...], preferred_element_type=F32)
    uv = jnp.dot(ext, wv_ref[...], preferred_element_type=F32)
    ug = _conv3_rows(ug, cg_ref[...], tm, halo) + bg_ref[...]
    uv = _conv3_rows(uv, cv_ref[...], tm, halo) + bv_ref[...]
    act = (ug / (1.0 + jnp.exp(-ug))) * uv
    acc_ref[...] += jnp.dot(act.astype(BF16), wd_ref[...], preferred_element_type=F32)

    @pl.when(j == pl.num_programs(1) - 1)
    def _():
        y = x1_ref[...] + acc_ref[...]
        ms = jnp.mean(y * y, axis=-1, keepdims=True)
        o_ref[...] = y * lax.rsqrt(ms + NORM_EPS) * gf_ref[...]


def _ffn(hn2, x1, w_up, conv_w, conv_b, w_down, norm_f, seq_len, tm, tf):
    t, d = x1.shape
    dff = w_down.shape[0]
    nj = dff // tf
    halo = BF16_SUBLANES
    r = tm // halo
    nb = t // halo
    kern = functools.partial(_ffn_kernel, blocks_per_seq=seq_len // tm)
    cb = conv_b.reshape(1, 2 * dff)
    return pl.pallas_call(
        kern,
        grid=(t // tm, nj),
        in_specs=[pl.BlockSpec((halo, d), lambda i, j: (jnp.maximum(i * r - 1, 0), 0)),
                  pl.BlockSpec((tm, d), lambda i, j: (i, 0)),
                  pl.BlockSpec((halo, d), lambda i, j: (jnp.minimum((i + 1) * r, nb - 1), 0)),
                  pl.BlockSpec((tm, d), lambda i, j: (i, 0)),
                  pl.BlockSpec((d, tf), lambda i, j: (0, j)),
                  pl.BlockSpec((d, tf), lambda i, j: (0, nj + j)),
                  pl.BlockSpec((3, tf), lambda i, j: (0, j)),
                  pl.BlockSpec((3, tf), lambda i, j: (0, nj + j)),
                  pl.BlockSpec((1, tf), lambda i, j: (0, j)),
                  pl.BlockSpec((1, tf), lambda i, j: (0, nj + j)),
                  pl.BlockSpec((tf, d), lambda i, j: (j, 0)),
                  pl.BlockSpec((1, d), lambda i, j: (0, 0))],
        out_specs=pl.BlockSpec((tm, d), lambda i, j: (i, 0)),
        out_shape=jax.ShapeDtypeStruct((t, d), F32),
        scratch_shapes=[pltpu.VMEM((tm + 2 * halo, d), BF16), pltpu.VMEM((tm, d), F32)],
        compiler_params=_cparams("parallel", "arbitrary"),
        name="conv_glu_mlp",
    )(hn2, hn2, hn2, x1, w_up, w_up, conv_w, conv_w, cb, cb, w_down, norm_f.reshape(1, d))


def _tiles(seq_len, d_model, d_ff):
    return dict(
        tm_proj=_pick(seq_len, 1024),
        tn_proj=2048,
        tq=_pick(seq_len, 512),
        tk=_pick(seq_len, 1024),
        tm_prep=_pick(seq_len, 512),
        tl_filt=_pick(seq_len, 512),
        tm2_dft=BF16_SUBLANES,
        tc_dft=256,
        tm_merge=_pick(seq_len, 512),
        tm_ffn=_pick(seq_len, 512),
        tf_ffn=_pick(d_ff, 512),
    )


def _encoder_layer(x, batch, seq_len, lambda_init, p, consts):
    t, d = x.shape
    tl = _tiles(seq_len, d, p["w_down"].shape[0])
    qk, hn = _proj_rope(x, p["norm1"], p["w_qk"], consts["cos"], consts["sin"], seq_len, tl["tm_proj"])
    v = _proj_plain(hn, p["w_v"], tl["tm_proj"], _pick(p["w_v"].shape[1], tl["tn_proj"]))
    hy = _proj_plain(hn, p["w_hy"], tl["tm_proj"], _pick(p["w_hy"].shape[1], tl["tn_proj"]))
    gates = _proj_gate(hn, p["w_gl"], p["gate_b"], tl["tm_proj"], _pick(p["w_gl"].shape[1], tl["tn_proj"]))

    attn = _attention(qk.reshape(batch, seq_len, -1), v.reshape(batch, seq_len, -1),
                      p["lambda_q1"], p["lambda_k1"], p["lambda_q2"], p["lambda_k2"], p["subln_g"],
                      batch, seq_len, lambda_init, tl["tq"], tl["tk"]).reshape(t, ATTN_WIDTH)

    u, x0 = _hyena_prep(hy, p["in_conv_w"], p["in_conv_b"], seq_len, tl["tm_prep"])
    circ = _hyena_filters(seq_len, p["filt_w1"], p["filt_b1"], p["filt_w2"], p["filt_b2"],
                          p["filt_w3"], p["filt_b3"], p["filt_w4"], p["filt_freq"], tl["tl_filt"])
    hyena = _hyena_long_conv(u, x0, circ, p["hyena_bias"], batch, seq_len, consts["dft"],
                             tl["tm2_dft"], tl["tc_dft"])

    x1, hn2 = _merge(attn, hyena, gates, x, p["w_attn_out"], p["w_hyena_out"], p["w_out"], p["norm2"],
                     tl["tm_merge"])
    return x1, hn2, tl


def kernel(x_prompt, x_sample, norm1, w_in, in_conv_w, in_conv_b, gate_b, lambda_q1, lambda_k1, lambda_q2, lambda_k2, subln_g, filt_w1, filt_b1, filt_w2, filt_b2, filt_w3, filt_b3, filt_w4, filt_freq, hyena_bias, w_attn_out, w_hyena_out, w_out, norm2, w_up, ffn_conv_w, ffn_conv_b, w_down, norm_f):
    depth = norm1.shape[0]
    assert depth == 1, "this kernel implements the single-layer trunk"
    d_model = x_prompt.shape[-1]
    hyena_width = hyena_bias.shape[-1]
    o1 = ATTN_QK_WIDTH
    o2 = o1 + ATTN_QK_WIDTH
    o3 = o2 + ATTN_WIDTH
    o4 = o3 + 3 * hyena_width

    layers = []
    for i in range(depth):
        wi = w_in[i].astype(BF16)
        layers.append(dict(
            norm1=norm1[i], w_qk=wi[:, :o2], w_v=wi[:, o2:o3], w_hy=wi[:, o3:o4], w_gl=wi[:, o4:],
            in_conv_w=in_conv_w[i], in_conv_b=in_conv_b[i], gate_b=gate_b[i],
            lambda_q1=lambda_q1[i], lambda_k1=lambda_k1[i], lambda_q2=lambda_q2[i], lambda_k2=lambda_k2[i],
            subln_g=subln_g[i],
            filt_w1=filt_w1[i], filt_b1=filt_b1[i], filt_w2=filt_w2[i], filt_b2=filt_b2[i],
            filt_w3=filt_w3[i], filt_b3=filt_b3[i], filt_w4=filt_w4[i], filt_freq=filt_freq[i],
            hyena_bias=hyena_bias[i],
            w_attn_out=w_attn_out[i].astype(BF16), w_hyena_out=w_hyena_out[i].astype(BF16),
            w_out=w_out[i].astype(BF16), norm2=norm2[i],
            w_up=w_up[i].astype(BF16), ffn_conv_w=ffn_conv_w[i], ffn_conv_b=ffn_conv_b[i],
            w_down=w_down[i].astype(BF16)))

    def trunk(x3):
        batch, seq_len, _ = x3.shape
        cos, sin = _rope_tables(seq_len)
        consts = dict(cos=cos, sin=sin, dft=_dft_tables(seq_len))
        x = x3.reshape(batch * seq_len, d_model)
        lambda_init = 0.8 - 0.6 * math.exp(-0.3 * 0)
        p = layers[0]
        x1, hn2, tl = _encoder_layer(x, batch, seq_len, lambda_init, p, consts)
        y = _ffn(hn2, x1, p["w_up"], p["ffn_conv_w"], p["ffn_conv_b"], p["w_down"], norm_f, seq_len,
                 tl["tm_ffn"], tl["tf_ffn"])
        return y.reshape(batch, seq_len, d_model)

    return (trunk(x_prompt), trunk(x_sample))
```
